```python
import math
import jax, jax.numpy as jnp
from jax import lax
import numpy as np

D_MODEL = 1024
BATCH = 32
SEQ = 2048
DEPTH = 1

D_SSM = D_MODEL // 2
SSM_HEAD_DIM = 64
H_SSM = D_SSM // SSM_HEAD_DIM
SSM_GROUPS = 2
SSM_HPG = H_SSM // SSM_GROUPS
D_STATE = 128
CONV_K = 4
CHUNK = 128
D_CONV = D_SSM + 2 * SSM_GROUPS * D_STATE
D_FOX = D_MODEL // 2
FOX_HEAD_DIM = 64
H_FOX = D_FOX // FOX_HEAD_DIM
Q_BLOCK = 128
D_MIX = D_SSM + D_FOX
D_IN = D_SSM + D_CONV + H_SSM + 3 * D_FOX + H_FOX
D_FF = 128 * ((8 * D_MODEL // 3 + 127) // 128)
D_PLE = 256
EPS = 1e-6
DT_MIN = 0.001
DT_MAX = 0.1

kernel_name = "hybrid_ssd_fox_macaron_ple"


def rms_norm(x, w):
    xf = x.astype(jnp.float32)
    y = xf * lax.rsqrt(jnp.mean(xf * xf, axis=-1, keepdims=True) + EPS)
    return (y * w.astype(jnp.float32)).astype(x.dtype)


def gated_group_rms_norm(y, z, w):
    g = (y * jax.nn.silu(z)).astype(jnp.float32)
    shp = g.shape
    g = g.reshape(shp[:-1] + (SSM_GROUPS, shp[-1] // SSM_GROUPS))
    g = g * lax.rsqrt(jnp.mean(g * g, axis=-1, keepdims=True) + EPS)
    return (g.reshape(shp) * w.astype(jnp.float32)).astype(y.dtype)


def swiglu(h, w_gate, w_up, w_down):
    return (jax.nn.silu(h @ w_gate) * (h @ w_up)) @ w_down


def causal_depthwise_conv(x, w, b):
    y = lax.conv_general_dilated(
        x, w[:, None, :].astype(x.dtype), window_strides=(1,),
        padding=[(CONV_K - 1, 0)], dimension_numbers=('NWC', 'WIO', 'NWC'),
        feature_group_count=x.shape[-1])
    return y + b.astype(x.dtype)


def ssd_chunked_scan(xs, dt, a, b_mat, c_mat):
    bsz, seq = xs.shape[0], xs.shape[1]
    nc = seq // CHUNK
    x_c = (xs * dt[..., None]).reshape(bsz, nc, CHUNK, SSM_GROUPS, SSM_HPG, SSM_HEAD_DIM)
    a_c = (dt * a).reshape(bsz, nc, CHUNK, SSM_GROUPS, SSM_HPG)
    b_c = b_mat.reshape(bsz, nc, CHUNK, SSM_GROUPS, D_STATE)
    c_c = c_mat.reshape(bsz, nc, CHUNK, SSM_GROUPS, D_STATE)
    a_cs = jnp.cumsum(a_c, axis=2)
    a_cs_t = jnp.moveaxis(a_cs, 2, -1)
    causal = jnp.tril(jnp.ones((CHUNK, CHUNK), dtype=bool))
    seg = a_cs_t[..., :, None] - a_cs_t[..., None, :]
    decay_in = jnp.exp(jnp.where(causal, seg, -jnp.inf))
    cb = jnp.einsum('bclgn,bcsgn->bcgls', c_c, b_c)
    y_diag = jnp.einsum('bcgls,bcgels,bcsgep->bclgep', cb, decay_in, x_c)
    decay_to_end = jnp.exp(a_cs[:, :, -1:] - a_cs)
    chunk_states = jnp.einsum('bclgn,bclge,bclgep->bcgepn', b_c, decay_to_end, x_c)
    chunk_decay = jnp.exp(a_cs[:, :, -1])

    def step(state, inp):
        st, dec = inp
        return state * dec[..., None, None] + st, state

    init = jnp.zeros_like(chunk_states[:, 0])
    _, prev_states = lax.scan(
        step, init, (jnp.moveaxis(chunk_states, 1, 0), jnp.moveaxis(chunk_decay, 1, 0)))
    prev_states = jnp.moveaxis(prev_states, 0, 1)
    y_off = jnp.einsum('bclgn,bcgepn,bclge->bclgep', c_c, prev_states, jnp.exp(a_cs))
    return (y_diag + y_off).reshape(bsz, seq, SSM_GROUPS, SSM_HPG, SSM_HEAD_DIM)


def forgetting_attention(q, k, v, log_f):
    seq = q.shape[1]
    cum = jnp.moveaxis(jnp.cumsum(log_f, axis=1), 1, 2)
    scale = FOX_HEAD_DIM ** -0.5
    outs = []
    for blk in range(seq // Q_BLOCK):
        q0 = blk * Q_BLOCK
        kend = q0 + Q_BLOCK
        logits = jnp.einsum('bqhd,bkhd->bhqk', q[:, q0:kend], k[:, :kend]).astype(jnp.float32) * scale
        bias = cum[:, :, q0:kend, None] - cum[:, :, None, :kend]
        mask = jnp.arange(kend)[None, :] <= (q0 + jnp.arange(Q_BLOCK))[:, None]
        logits = jnp.where(mask, logits + bias, -jnp.inf)
        probs = jax.nn.softmax(logits, axis=-1).astype(v.dtype)
        outs.append(jnp.einsum('bhqk,bkhd->bqhd', probs, v[:, :kend]))
    return jnp.concatenate(outs, axis=1)


def setup_inputs(seed: int = 0) -> dict:
    key = jax.random.key(seed)
    ks = jax.random.split(key, 26)

    def dense(k, shape, fan_in):
        return jax.random.normal(k, shape, jnp.float32) * fan_in ** -0.5

    def gain(k, shape):
        return 1.0 + 0.02 * jax.random.normal(k, shape, jnp.float32)

    dt = jnp.exp(jax.random.uniform(ks[10], (DEPTH, H_SSM), jnp.float32,
                                    minval=math.log(DT_MIN), maxval=math.log(DT_MAX)))
    dt_bias = dt + jnp.log(-jnp.expm1(-dt))
    return {
        'x': jax.random.normal(ks[0], (BATCH, SEQ, D_MODEL), jnp.float32),
        'p': jax.random.normal(ks[1], (DEPTH, BATCH, SEQ, D_PLE), jnp.float32),
        'ffn1_norm': gain(ks[2], (DEPTH, D_MODEL)),
        'ffn1_w_gate': dense(ks[3], (DEPTH, D_MODEL, D_FF), D_MODEL),
        'ffn1_w_up': dense(ks[4], (DEPTH, D_MODEL, D_FF), D_MODEL),
        'ffn1_w_down': dense(ks[5], (DEPTH, D_FF, D_MODEL), D_FF),
        'mix_norm': gain(ks[6], (DEPTH, D_MODEL)),
        'w_in': dense(ks[7], (DEPTH, D_MODEL, D_IN), D_MODEL),
        'conv_w': dense(ks[8], (DEPTH, CONV_K, D_CONV), CONV_K),
        'conv_b': 0.02 * jax.random.normal(ks[9], (DEPTH, D_CONV), jnp.float32),
        'dt_bias': dt_bias,
        'a_log': jnp.log(jax.random.uniform(ks[11], (DEPTH, H_SSM), jnp.float32, minval=1.0, maxval=16.0)),
        'd_skip': 1.0 + 0.1 * jax.random.normal(ks[12], (DEPTH, H_SSM), jnp.float32),
        'ssd_norm': gain(ks[13], (DEPTH, D_SSM)),
        'fox_f_bias': jax.random.uniform(ks[14], (DEPTH, H_FOX), jnp.float32, minval=1.0, maxval=4.0),
        'fox_norm': gain(ks[15], (DEPTH, D_FOX)),
        'w_out': dense(ks[16], (DEPTH, D_MIX, D_MODEL), D_MIX),
        'ffn2_norm': gain(ks[17], (DEPTH, D_MODEL)),
        'ffn2_w_gate': dense(ks[18], (DEPTH, D_MODEL, D_FF), D_MODEL),
        'ffn2_w_up': dense(ks[19], (DEPTH, D_MODEL, D_FF), D_MODEL),
        'ffn2_w_down': dense(ks[20], (DEPTH, D_FF, D_MODEL), D_FF),
        'ple_norm': gain(ks[21], (DEPTH, D_MODEL)),
        'ple_w_gate': dense(ks[22], (DEPTH, D_MODEL, D_MODEL), D_MODEL),
        'ple_w_up': dense(ks[23], (DEPTH, D_PLE, D_MODEL), D_PLE),
        'final_norm': gain(ks[24], (D_MODEL,)),
    }


def reference(x, p, ffn1_norm, ffn1_w_gate, ffn1_w_up, ffn1_w_down, mix_norm, w_in,
              conv_w, conv_b, dt_bias, a_log, d_skip, ssd_norm, fox_f_bias, fox_norm,
              w_out, ffn2_norm, ffn2_w_gate, ffn2_w_up, ffn2_w_down, ple_norm,
              ple_w_gate, ple_w_up, final_norm):
    bsz, seq = x.shape[0], x.shape[1]
    offs = np.cumsum([D_SSM, D_CONV, H_SSM, D_FOX, D_FOX, D_FOX]).tolist()
    h = x
    for i in range(DEPTH):
        h = h + 0.5 * swiglu(rms_norm(h, ffn1_norm[i]), ffn1_w_gate[i], ffn1_w_up[i], ffn1_w_down[i])

        u = rms_norm(h, mix_norm[i])
        proj = u @ w_in[i]
        z, xbc, dt_raw, q, k, v, f_raw = jnp.split(proj, offs, axis=-1)

        xbc = jax.nn.silu(causal_depthwise_conv(xbc, conv_w[i], conv_b[i]))
        xs, b_mat, c_mat = jnp.split(xbc, [D_SSM, D_SSM + SSM_GROUPS * D_STATE], axis=-1)
        xs = xs.reshape(bsz, seq, SSM_GROUPS, SSM_HPG, SSM_HEAD_DIM)
        b_mat = b_mat.reshape(bsz, seq, SSM_GROUPS, D_STATE)
        c_mat = c_mat.reshape(bsz, seq, SSM_GROUPS, D_STATE)
        dt = jax.nn.softplus(dt_raw.astype(jnp.float32) + dt_bias[i].astype(jnp.float32))
        dt = dt.reshape(bsz, seq, SSM_GROUPS, SSM_HPG)
        a = -jnp.exp(a_log[i].astype(jnp.float32)).reshape(SSM_GROUPS, SSM_HPG)
        y = ssd_chunked_scan(xs, dt, a, b_mat, c_mat)
        y = y + d_skip[i].reshape(SSM_GROUPS, SSM_HPG)[..., None] * xs
        y_ssd = gated_group_rms_norm(y.reshape(bsz, seq, D_SSM).astype(z.dtype), z, ssd_norm[i])

        q = q.reshape(bsz, seq, H_FOX, FOX_HEAD_DIM)
        k = k.reshape(bsz, seq, H_FOX, FOX_HEAD_DIM)
        v = v.reshape(bsz, seq, H_FOX, FOX_HEAD_DIM)
        log_f = jax.nn.log_sigmoid(f_raw.astype(jnp.float32) + fox_f_bias[i].astype(jnp.float32))
        y_fox = forgetting_attention(q, k, v, log_f).reshape(bsz, seq, D_FOX)
        y_fox = rms_norm(y_fox, fox_norm[i])

        h = h + jnp.concatenate([y_ssd, y_fox], axis=-1) @ w_out[i]

        h = h + 0.5 * swiglu(rms_norm(h, ffn2_norm[i]), ffn2_w_gate[i], ffn2_w_up[i], ffn2_w_down[i])

        gate = jax.nn.sigmoid(rms_norm(h, ple_norm[i]) @ ple_w_gate[i])
        h = h + gate * (p[i] @ ple_w_up[i])
    return rms_norm(h, final_norm)
```

```python
import functools

import jax
import jax.numpy as jnp
from jax import lax
from jax.experimental import pallas as pl
from jax.experimental.pallas import tpu as pltpu

F32 = jnp.float32
BF16 = jnp.bfloat16

D_MODEL = 1024
D_SSM = 512
HEAD_DIM = 64
N_HEADS = 8
SSM_GROUPS = 2
D_STATE = 128
CONV_K = 4
CHUNK = 128
D_CONV = D_SSM + 2 * SSM_GROUPS * D_STATE
D_FOX = 512
D_FF = 2816
D_PLE = 256
EPS = 1e-6

LANES = 128
FF_CHUNK = 256
N_FF_CHUNKS = D_FF // FF_CHUNK
TOKEN_TILE = 512
SEQ_TILE = 512
Q_TILE = 256
D_PROJ = D_SSM + D_CONV + 3 * D_FOX + LANES
VMEM_LIMIT = 56 * 1024 * 1024

AUX_DT = 0
AUX_ACS = 8
AUX_CUM = 16
AUX_ROWS = 32


def _rms(x, w):
    ms = jnp.mean(x * x, axis=-1, keepdims=True)
    return x * lax.rsqrt(ms + EPS) * w


def _dot(a, b):
    return jnp.dot(a, b, preferred_element_type=F32)


def _dot_nt(a, b):
    return lax.dot_general(a, b, (((1,), (1,)), ((), ())), preferred_element_type=F32)


def _swiglu_into(xn, wgu_ref, wd_ref, acc_ref):
    for j in range(N_FF_CHUNKS):
        gu = _dot(xn, wgu_ref[:, 2 * FF_CHUNK * j:2 * FF_CHUNK * (j + 1)])
        g = gu[:, :FF_CHUNK]
        u = gu[:, FF_CHUNK:]
        a = (g * jax.nn.sigmoid(g) * u).astype(BF16)
        d = _dot(a, wd_ref[FF_CHUNK * j:FF_CHUNK * (j + 1), :])
        if j == 0:
            acc_ref[...] = d
        else:
            acc_ref[...] += d


def _ffn1_kernel(x_ref, nw_ref, wgu_ref, wd_ref, o_ref, acc_ref):
    x = x_ref[...]
    xn = _rms(x, nw_ref[...]).astype(BF16)
    _swiglu_into(xn, wgu_ref, wd_ref, acc_ref)
    o_ref[...] = x + 0.5 * acc_ref[...]


def _const_spec(shape):
    nd = len(shape)
    return pl.BlockSpec(shape, lambda *_: (0,) * nd, pipeline_mode=pl.Buffered(1))


def _ffn1(x2d, nw, wgu, wd):
    t = x2d.shape[0]
    return pl.pallas_call(
        _ffn1_kernel,
        grid=(t // TOKEN_TILE,),
        in_specs=[
            pl.BlockSpec((TOKEN_TILE, D_MODEL), lambda i: (i, 0)),
            _const_spec((1, D_MODEL)),
            _const_spec((D_MODEL, 2 * D_FF)),
            _const_spec((D_FF, D_MODEL)),
        ],
        out_specs=pl.BlockSpec((TOKEN_TILE, D_MODEL), lambda i: (i, 0)),
        out_shape=jax.ShapeDtypeStruct((t, D_MODEL), F32),
        scratch_shapes=[pltpu.VMEM((TOKEN_TILE, D_MODEL), F32)],
        compiler_params=pltpu.CompilerParams(
            dimension_semantics=("arbitrary",), vmem_limit_bytes=VMEM_LIMIT),
        name="ffn1",
    )(x2d, nw, wgu, wd)


def _split3(x):
    hi = x.astype(BF16)
    r1 = x - hi.astype(F32)
    mid = r1.astype(BF16)
    lo = (r1 - mid.astype(F32)).astype(BF16)
    return hi, mid, lo


def _inproj_kernel(h_ref, nw_ref, win_ref, cw_ref, cb_ref, bias_ref, arow_ref,
                   z_ref, xbc_ref, q_ref, k_ref, v_ref, aux_ref, auxt_ref,
                   tail_ref, run_ref):
    s_idx = pl.program_id(1)

    @pl.when(s_idx == 0)
    def _():
        tail_ref[...] = jnp.zeros_like(tail_ref)
        run_ref[...] = jnp.zeros_like(run_ref)

    u = _rms(h_ref[...], nw_ref[...]).astype(BF16)
    c0 = 0
    z_ref[...] = _dot(u, win_ref[:, c0:c0 + D_SSM]).astype(BF16)
    c0 += D_SSM
    raw = _dot(u, win_ref[:, c0:c0 + D_CONV])
    c0 += D_CONV
    q_ref[...] = _dot(u, win_ref[:, c0:c0 + D_FOX]).astype(BF16)
    c0 += D_FOX
    k_ref[...] = _dot(u, win_ref[:, c0:c0 + D_FOX]).astype(BF16)
    c0 += D_FOX
    v_ref[...] = _dot(u, win_ref[:, c0:c0 + D_FOX]).astype(BF16)
    c0 += D_FOX
    small = _dot(u, win_ref[:, c0:c0 + LANES])

    prev = tail_ref[...]
    row8 = lax.broadcasted_iota(jnp.int32, (8, D_CONV), 0)
    y = raw * cw_ref[CONV_K - 1:CONV_K, :] + cb_ref[...]
    for sh in range(1, CONV_K):
        rolled = pltpu.roll(raw, sh, axis=0)
        head = jnp.where(row8 < sh, pltpu.roll(prev, sh, axis=0), rolled[0:8, :])
        shifted = jnp.concatenate([head, rolled[8:, :]], axis=0)
        y = y + shifted * cw_ref[CONV_K - 1 - sh:CONV_K - sh, :]
    tail_ref[...] = raw[SEQ_TILE - 8:, :]
    xbc_ref[...] = (y * jax.nn.sigmoid(y)).astype(BF16)

    sb = small + bias_ref[...]
    l1p = jnp.log1p(jnp.exp(-jnp.abs(sb)))
    softplus = jnp.maximum(sb, 0.0) + l1p
    log_sig = -(jnp.maximum(-sb, 0.0) + l1p)
    lane = lax.broadcasted_iota(jnp.int32, (SEQ_TILE, LANES), 1)
    dmat = jnp.where(lane < AUX_CUM, softplus * arow_ref[...],
                     jnp.where(lane < AUX_CUM + N_HEADS, log_sig, 0.0))
    r = lax.broadcasted_iota(jnp.int32, (CHUNK, CHUNK), 0)
    c = lax.broadcasted_iota(jnp.int32, (CHUNK, CHUNK), 1)
    tri = (r >= c).astype(BF16)
    lane_c = lax.broadcasted_iota(jnp.int32, (CHUNK, LANES), 1)
    lane_r = lax.broadcasted_iota(jnp.int32, (1, LANES), 1)
    run = run_ref[0:1, :]
    pieces = []
    for ci in range(SEQ_TILE // CHUNK):
        rows = slice(CHUNK * ci, CHUNK * (ci + 1))
        hi, mid, lo = _split3(dmat[rows, :])
        cs = _dot(tri, hi) + _dot(tri, mid) + _dot(tri, lo)
        pieces.append(jnp.where(lane_c < AUX_ACS, softplus[rows, :], cs + run))
        run = run + jnp.where(lane_r >= AUX_CUM, cs[CHUNK - 1:CHUNK, :], 0.0)
    run_ref[...] = jnp.broadcast_to(run, run_ref.shape)
    aux = jnp.concatenate(pieces, axis=0)
    aux_ref[...] = aux
    auxt_ref[...] = aux.T[0:AUX_ROWS, :]


def _inproj(h1, nw, win, cw, cb, bias_row, a_row):
    b, s, _ = h1.shape
    tile = lambda d: pl.BlockSpec((None, SEQ_TILE, d), lambda bi, si: (bi, si, 0))
    outs = [
        jax.ShapeDtypeStruct((b, s, D_SSM), BF16),
        jax.ShapeDtypeStruct((b, s, D_CONV), BF16),
        jax.ShapeDtypeStruct((b, s, D_FOX), BF16),
        jax.ShapeDtypeStruct((b, s, D_FOX), BF16),
        jax.ShapeDtypeStruct((b, s, D_FOX), BF16),
        jax.ShapeDtypeStruct((b, s, LANES), F32),
        jax.ShapeDtypeStruct((b, AUX_ROWS, s), F32),
    ]
    return pl.pallas_call(
        _inproj_kernel,
        grid=(b, s // SEQ_TILE),
        in_specs=[
            tile(D_MODEL),
            _const_spec((1, D_MODEL)),
            _const_spec((D_MODEL, D_PROJ)),
            _const_spec((CONV_K, D_CONV)),
            _const_spec((1, D_CONV)),
            _const_spec((1, LANES)),
            _const_spec((1, LANES)),
        ],
        out_specs=[tile(D_SSM), tile(D_CONV), tile(D_FOX), tile(D_FOX), tile(D_FOX),
                   tile(LANES),
                   pl.BlockSpec((None, AUX_ROWS, SEQ_TILE), lambda bi, si: (bi, 0, si))],
        out_shape=outs,
        scratch_shapes=[pltpu.VMEM((8, D_CONV), F32), pltpu.VMEM((8, LANES), F32)],
        compiler_params=pltpu.CompilerParams(
            dimension_semantics=("arbitrary", "arbitrary"), vmem_limit_bytes=VMEM_LIMIT),
        name="inproj",
    )(h1, nw, win, cw, cb, bias_row, a_row)


def _ssd_kernel(xbc_ref, z_ref, aux_ref, auxt_ref, dskip_ref, nw_ref, o_ref, st_ref):
    st_ref[...] = jnp.zeros_like(st_ref)
    n_chunks = xbc_ref.shape[0] // CHUNK
    r = lax.broadcasted_iota(jnp.int32, (CHUNK, CHUNK), 0)
    c = lax.broadcasted_iota(jnp.int32, (CHUNK, CHUNK), 1)
    causal = r >= c
    lo_half = c < HEAD_DIM
    hpg = N_HEADS // SSM_GROUPS
    gw = D_SSM // SSM_GROUPS

    def chunk_body(ci, carry):
        l0 = pl.multiple_of(ci * CHUNK, CHUNK)
        aux = aux_ref[pl.ds(l0, CHUNK), :]
        auxt = auxt_ref[:, pl.ds(l0, CHUNK)]
        ys = []
        for g in range(SSM_GROUPS):
            b_g = xbc_ref[pl.ds(l0, CHUNK), D_SSM + D_STATE * g:D_SSM + D_STATE * (g + 1)]
            c_g = xbc_ref[pl.ds(l0, CHUNK),
                          D_SSM + D_STATE * (SSM_GROUPS + g):D_SSM + D_STATE * (SSM_GROUPS + g + 1)]
            cb = _dot_nt(c_g, b_g)
            b_gt = b_g.astype(F32).T.astype(BF16)
            for pp in range(hpg // 2):
                pair = g * (hpg // 2) + pp
                xs_bf = xbc_ref[pl.ds(l0, CHUNK), LANES * pair:LANES * (pair + 1)]
                xs = xs_bf.astype(F32)
                y_d, w_col, e_col, cd = [], [], [], []
                for e in range(2):
                    he = 2 * pair + e
                    acs_col = aux[:, AUX_ACS + he:AUX_ACS + he + 1]
                    dt_col = aux[:, AUX_DT + he:AUX_DT + he + 1]
                    acs_last = aux[CHUNK - 1:CHUNK, AUX_ACS + he:AUX_ACS + he + 1]
                    acs_row = auxt[AUX_ACS + he:AUX_ACS + he + 1, :]
                    dt_row = auxt[AUX_DT + he:AUX_DT + he + 1, :]
                    decay = jnp.exp(jnp.where(causal, acs_col - acs_row, -jnp.inf))
                    m = (cb * decay * dt_row).astype(BF16)
                    y_d.append(_dot(m, xs_bf))
                    w_col.append(dt_col * jnp.exp(acs_last - acs_col))
                    e_col.append(jnp.exp(acs_col))
                    cd.append(jnp.exp(acs_last))
                st = st_ref[pair]
                y_off = _dot(c_g, st.astype(BF16)) * jnp.where(lo_half, e_col[0], e_col[1])
                xs_w = (xs * jnp.where(lo_half, w_col[0], w_col[1])).astype(BF16)
                st_ref[pair] = st * jnp.where(lo_half, cd[0], cd[1]) + _dot(b_gt, xs_w)
                y = jnp.where(lo_half, y_d[0], y_d[1]) + y_off
                y = y + dskip_ref[:, LANES * pair:LANES * (pair + 1)] * xs
                ys.append(y)
        outs = []
        for g in range(SSM_GROUPS):
            yg = jnp.concatenate(ys[g * (hpg // 2):(g + 1) * (hpg // 2)], axis=1)
            zg = z_ref[pl.ds(l0, CHUNK), gw * g:gw * (g + 1)].astype(F32)
            gg = yg * (zg * jax.nn.sigmoid(zg))
            gg = gg * lax.rsqrt(jnp.mean(gg * gg, axis=-1, keepdims=True) + EPS)
            outs.append(gg * nw_ref[:, gw * g:gw * (g + 1)])
        o_ref[pl.ds(l0, CHUNK), :] = jnp.concatenate(outs, axis=1).astype(BF16)
        return carry

    lax.fori_loop(0, n_chunks, chunk_body, 0)


def _ssd(xbc, z, aux, auxt, dskip_row, nw):
    b, s, _ = xbc.shape
    full = lambda r, d: pl.BlockSpec((None, r, d), lambda bi: (bi, 0, 0))
    return pl.pallas_call(
        _ssd_kernel,
        grid=(b,),
        in_specs=[full(s, D_CONV), full(s, D_SSM), full(s, LANES), full(AUX_ROWS, s),
                  _const_spec((1, D_SSM)), _const_spec((1, D_SSM))],
        out_specs=full(s, D_SSM),
        out_shape=jax.ShapeDtypeStruct((b, s, D_SSM), BF16),
        scratch_shapes=[pltpu.VMEM((N_HEADS // 2, D_STATE, LANES), F32)],
        compiler_params=pltpu.CompilerParams(
            dimension_semantics=("arbitrary",), vmem_limit_bytes=VMEM_LIMIT),
        name="ssd",
    )(xbc, z, aux, auxt, dskip_row, nw)


def _fox_kernel(q_ref, k_ref, v_ref, auxt_ref, nw_ref, o_ref):
    qi = pl.program_id(1)
    q0 = pl.multiple_of(qi * Q_TILE, Q_TILE)
    r = lax.broadcasted_iota(jnp.int32, (Q_TILE, Q_TILE), 0)
    c = lax.broadcasted_iota(jnp.int32, (Q_TILE, Q_TILE), 1)
    causal = r >= c
    lane = lax.broadcasted_iota(jnp.int32, (Q_TILE, LANES), 1)
    lo_half = lane < HEAD_DIM
    pair_outs = []
    for pair in range(N_HEADS // 2):
        cols = slice(LANES * pair, LANES * (pair + 1))
        q_pair = q_ref[:, cols].astype(F32)
        head_outs = []
        for e in range(2):
            he = 2 * pair + e
            qm = jnp.where(lo_half if e == 0 else jnp.logical_not(lo_half),
                           q_pair, 0.0).astype(BF16)

            def block(k0, mask):
                kb = k_ref[pl.ds(k0, Q_TILE), cols]
                vb = v_ref[pl.ds(k0, Q_TILE), cols]
                ck = auxt_ref[AUX_CUM + he:AUX_CUM + he + 1, pl.ds(k0, Q_TILE)]
                s = _dot_nt(qm, kb) - ck
                if mask:
                    s = jnp.where(causal, s, -jnp.inf)
                return s, vb

            s, vb = block(q0, True)
            m0 = jnp.max(s, axis=-1, keepdims=True)
            p0 = jnp.exp(s - m0)
            l0 = jnp.sum(p0, axis=-1, keepdims=True)
            acc0 = _dot(p0.astype(BF16), vb)

            def body(j, carry):
                m, l, acc = carry
                s, vb = block(pl.multiple_of(j * Q_TILE, Q_TILE), False)
                m_new = jnp.maximum(m, jnp.max(s, axis=-1, keepdims=True))
                alpha = jnp.exp(m - m_new)
                p = jnp.exp(s - m_new)
                l = alpha * l + jnp.sum(p, axis=-1, keepdims=True)
                acc = alpha * acc + _dot(p.astype(BF16), vb)
                return m_new, l, acc

            m, l, acc = lax.fori_loop(0, qi, body, (m0, l0, acc0))
            head_outs.append(acc / l)
        pair_outs.append(jnp.where(lo_half, head_outs[0], head_outs[1]))
    y = jnp.concatenate(pair_outs, axis=1)
    o_ref[...] = _rms(y, nw_ref[...]).astype(BF16)


def _fox(q, k, v, auxt, nw):
    b, s, _ = q.shape
    return pl.pallas_call(
        _fox_kernel,
        grid=(b, s // Q_TILE),
        in_specs=[
            pl.BlockSpec((None, Q_TILE, D_FOX), lambda bi, qi: (bi, qi, 0)),
            pl.BlockSpec((None, s, D_FOX), lambda bi, qi: (bi, 0, 0)),
            pl.BlockSpec((None, s, D_FOX), lambda bi, qi: (bi, 0, 0)),
            pl.BlockSpec((None, AUX_ROWS, s), lambda bi, qi: (bi, 0, 0)),
            _const_spec((1, D_FOX)),
        ],
        out_specs=pl.BlockSpec((None, Q_TILE, D_FOX), lambda bi, qi: (bi, qi, 0)),
        out_shape=jax.ShapeDtypeStruct((b, s, D_FOX), BF16),
        compiler_params=pltpu.CompilerParams(
            dimension_semantics=("arbitrary", "arbitrary"), vmem_limit_bytes=VMEM_LIMIT),
        name="fox",
    )(q, k, v, auxt, nw)


def _tail_kernel(h_ref, ys_ref, yf_ref, p_ref, wos_ref, wof_ref, n2_ref, wgu_ref, wd_ref,
                 pn_ref, pg_ref, pu_ref, fn_ref, o_ref, acc_ref):
    h2 = h_ref[...] + _dot(ys_ref[...], wos_ref[...]) + _dot(yf_ref[...], wof_ref[...])
    xn = _rms(h2, n2_ref[...]).astype(BF16)
    _swiglu_into(xn, wgu_ref, wd_ref, acc_ref)
    h3 = h2 + 0.5 * acc_ref[...]
    gate = jax.nn.sigmoid(_dot(_rms(h3, pn_ref[...]).astype(BF16), pg_ref[...]))
    h4 = h3 + gate * _dot(p_ref[...].astype(BF16), pu_ref[...])
    o_ref[...] = _rms(h4, fn_ref[...])


def _tail(h1, ys, yf, p2d, wos, wof, n2, wgu, wd, pn, pg, pu, fn):
    t = h1.shape[0]
    row = lambda d: pl.BlockSpec((TOKEN_TILE, d), lambda i: (i, 0))
    return pl.pallas_call(
        _tail_kernel,
        grid=(t // TOKEN_TILE,),
        in_specs=[
            row(D_MODEL), row(D_SSM), row(D_FOX), row(D_PLE),
            _const_spec((D_SSM, D_MODEL)), _const_spec((D_FOX, D_MODEL)),
            _const_spec((1, D_MODEL)),
            _const_spec((D_MODEL, 2 * D_FF)), _const_spec((D_FF, D_MODEL)),
            _const_spec((1, D_MODEL)), _const_spec((D_MODEL, D_MODEL)),
            _const_spec((D_PLE, D_MODEL)), _const_spec((1, D_MODEL)),
        ],
        out_specs=row(D_MODEL),
        out_shape=jax.ShapeDtypeStruct((t, D_MODEL), F32),
        scratch_shapes=[pltpu.VMEM((TOKEN_TILE, D_MODEL), F32)],
        compiler_params=pltpu.CompilerParams(
            dimension_semantics=("arbitrary",), vmem_limit_bytes=VMEM_LIMIT),
        name="tail",
    )(h1, ys, yf, p2d, wos, wof, n2, wgu, wd, pn, pg, pu, fn)


def _interleave_gate_up(w_gate, w_up):
    d = w_gate.shape[0]
    g = w_gate.reshape(d, N_FF_CHUNKS, 1, FF_CHUNK)
    u = w_up.reshape(d, N_FF_CHUNKS, 1, FF_CHUNK)
    return jnp.concatenate([g, u], axis=2).reshape(d, 2 * D_FF).astype(BF16)


def _pad_lanes(v):
    return jnp.pad(v, (0, LANES - v.shape[0])).reshape(1, LANES).astype(F32)


def kernel(x, p, ffn1_norm, ffn1_w_gate, ffn1_w_up, ffn1_w_down, mix_norm, w_in, conv_w, conv_b, dt_bias, a_log, d_skip, ssd_norm, fox_f_bias, fox_norm, w_out, ffn2_norm, ffn2_w_gate, ffn2_w_up, ffn2_w_down, ple_norm, ple_w_gate, ple_w_up, final_norm):
    bsz, seq, _ = x.shape
    depth = p.shape[0]
    row = lambda v: v.reshape(1, -1).astype(F32)
    h = x.reshape(bsz * seq, D_MODEL)
    for i in range(depth):
        h1 = _ffn1(h, row(ffn1_norm[i]), _interleave_gate_up(ffn1_w_gate[i], ffn1_w_up[i]),
                   ffn1_w_down[i].astype(BF16))

        w = w_in[i]
        o_z, o_xbc, o_dt = D_SSM, D_SSM + D_CONV, D_SSM + D_CONV + N_HEADS
        o_q, o_k, o_v = o_dt + D_FOX, o_dt + 2 * D_FOX, o_dt + 3 * D_FOX
        w_dt, w_f = w[:, o_xbc:o_dt], w[:, o_v:]
        gates = jnp.concatenate(
            [w_dt, w_dt, w_f, jnp.zeros((D_MODEL, LANES - 3 * N_HEADS), w.dtype)], axis=1)
        win = jnp.concatenate(
            [w[:, :o_xbc], w[:, o_dt:o_q] * (HEAD_DIM ** -0.5), w[:, o_q:o_v], gates],
            axis=1).astype(BF16)
        bias_row = _pad_lanes(jnp.concatenate([dt_bias[i], dt_bias[i], fox_f_bias[i]]))
        a_neg = -jnp.exp(a_log[i].astype(F32))
        a_row = _pad_lanes(jnp.concatenate([jnp.zeros_like(a_neg), a_neg]))
        z, xbc, q, k, v, aux, auxt = _inproj(
            h1.reshape(bsz, seq, D_MODEL), row(mix_norm[i]), win,
            conv_w[i].astype(F32), row(conv_b[i]), bias_row, a_row)

        y_ssd = _ssd(xbc, z, aux, auxt, row(jnp.repeat(d_skip[i], HEAD_DIM)), row(ssd_norm[i]))
        y_fox = _fox(q, k, v, auxt, row(fox_norm[i]))

        wo = w_out[i].astype(BF16)
        assert depth == 1
        h = _tail(h1, y_ssd.reshape(bsz * seq, D_SSM), y_fox.reshape(bsz * seq, D_FOX),
                  p[i].reshape(bsz * seq, D_PLE), wo[:D_SSM], wo[D_SSM:], row(ffn2_norm[i]),
                  _interleave_gate_up(ffn2_w_gate[i], ffn2_w_up[i]), ffn2_w_down[i].astype(BF16),
                  row(ple_norm[i]), ple_w_gate[i].astype(BF16), ple_w_up[i].astype(BF16),
                  row(final_norm))
    return h.reshape(bsz, seq, D_MODEL)
```

```python
import functools

import jax
import jax.numpy as jnp
from jax import lax
from jax.experimental import pallas as pl
from jax.experimental.pallas import tpu as pltpu

F32 = jnp.float32
BF16 = jnp.bfloat16

D_MODEL = 1024
D_SSM = 512
HEAD_DIM = 64
N_HEADS = 8
SSM_GROUPS = 2
D_STATE = 128
CONV_K = 4
CHUNK = 128
D_CONV = D_SSM + 2 * SSM_GROUPS * D_STATE
D_FOX = 512
D_HEADS_PAD = 1024
D_FF = 2816
D_PLE = 256
EPS = 1e-6
LOG2_E = 1.4426950408889634

LANES = 128
FF_CHUNK = 256
N_FF_CHUNKS = D_FF // FF_CHUNK
TOKEN_TILE = 512
SEQ_TILE = 512
Q_TILE = 256
D_PROJ = D_SSM + D_CONV + 3 * D_FOX + LANES
VMEM_LIMIT = 56 * 1024 * 1024

AUX_DT = 0
AUX_ACS = 8
AUX_CUM = 16
AUX_ROWS = 32


def _rms(x, w):
    ms = jnp.mean(x * x, axis=-1, keepdims=True)
    return x * lax.rsqrt(ms + EPS) * w


def _dot(a, b):
    return jnp.dot(a, b, preferred_element_type=F32)


def _dot_nt(a, b):
    return lax.dot_general(a, b, (((1,), (1,)), ((), ())), preferred_element_type=F32)


def _swiglu_into(xn, wgu_ref, wd_ref, acc_ref):
    for j in range(N_FF_CHUNKS):
        gu = _dot(xn, wgu_ref[:, 2 * FF_CHUNK * j:2 * FF_CHUNK * (j + 1)])
        g = gu[:, :FF_CHUNK]
        u = gu[:, FF_CHUNK:]
        a = (g * jax.nn.sigmoid(g) * u).astype(BF16)
        d = _dot(a, wd_ref[FF_CHUNK * j:FF_CHUNK * (j + 1), :])
        if j == 0:
            acc_ref[...] = d
        else:
            acc_ref[...] += d


def _ffn1_kernel(x_ref, nw_ref, wgu_ref, wd_ref, o_ref, acc_ref):
    x = x_ref[...]
    xn = _rms(x, nw_ref[...]).astype(BF16)
    _swiglu_into(xn, wgu_ref, wd_ref, acc_ref)
    o_ref[...] = x + 0.5 * acc_ref[...]


def _const_spec(shape):
    nd = len(shape)
    return pl.BlockSpec(shape, lambda *_: (0,) * nd, pipeline_mode=pl.Buffered(1))


def _ffn1(x2d, nw, wgu, wd):
    t = x2d.shape[0]
    return pl.pallas_call(
        _ffn1_kernel,
        grid=(t // TOKEN_TILE,),
        in_specs=[
            pl.BlockSpec((TOKEN_TILE, D_MODEL), lambda i: (i, 0)),
            _const_spec((1, D_MODEL)),
            _const_spec((D_MODEL, 2 * D_FF)),
            _const_spec((D_FF, D_MODEL)),
        ],
        out_specs=pl.BlockSpec((TOKEN_TILE, D_MODEL), lambda i: (i, 0)),
        out_shape=jax.ShapeDtypeStruct((t, D_MODEL), F32),
        scratch_shapes=[pltpu.VMEM((TOKEN_TILE, D_MODEL), F32)],
        compiler_params=pltpu.CompilerParams(
            dimension_semantics=("arbitrary",), vmem_limit_bytes=VMEM_LIMIT),
        name="ffn1",
    )(x2d, nw, wgu, wd)


def _split3(x):
    hi = x.astype(BF16)
    r1 = x - hi.astype(F32)
    mid = r1.astype(BF16)
    lo = (r1 - mid.astype(F32)).astype(BF16)
    return hi, mid, lo


def _inproj_kernel(h_ref, nw_ref, win_ref, cw_ref, cb_ref, bias_ref, arow_ref,
                   z_ref, xbc_ref, q_ref, k_ref, v_ref, aux_ref, auxt_ref,
                   tail_ref, run_ref):
    s_idx = pl.program_id(1)

    @pl.when(s_idx == 0)
    def _():
        tail_ref[...] = jnp.zeros_like(tail_ref)
        run_ref[...] = jnp.zeros_like(run_ref)

    u = _rms(h_ref[...], nw_ref[...]).astype(BF16)
    c0 = 0
    z_ref[...] = _dot(u, win_ref[:, c0:c0 + D_SSM]).astype(BF16)
    c0 += D_SSM
    raw = _dot(u, win_ref[:, c0:c0 + D_CONV])
    c0 += D_CONV
    lane = lax.broadcasted_iota(jnp.int32, (SEQ_TILE, LANES), 1)
    lo_half = lane < HEAD_DIM
    for dst_ref, fill in ((q_ref, 0.0), (k_ref, 0.0), (v_ref, jnp.where(lane == HEAD_DIM, 1.0, 0.0))):
        full = _dot(u, win_ref[:, c0:c0 + D_FOX])
        c0 += D_FOX
        for pair in range(N_HEADS // 2):
            blk = full[:, LANES * pair:LANES * (pair + 1)]
            swapped = pltpu.roll(blk, HEAD_DIM, axis=1)
            dst_ref[:, LANES * 2 * pair:LANES * (2 * pair + 1)] = jnp.where(lo_half, blk, fill).astype(BF16)
            dst_ref[:, LANES * (2 * pair + 1):LANES * (2 * pair + 2)] = jnp.where(lo_half, swapped, fill).astype(BF16)
    small = _dot(u, win_ref[:, c0:c0 + LANES])

    prev = tail_ref[...]
    row8 = lax.broadcasted_iota(jnp.int32, (8, D_CONV), 0)
    y = raw * cw_ref[CONV_K - 1:CONV_K, :] + cb_ref[...]
    for sh in range(1, CONV_K):
        rolled = pltpu.roll(raw, sh, axis=0)
        head = jnp.where(row8 < sh, pltpu.roll(prev, sh, axis=0), rolled[0:8, :])
        shifted = jnp.concatenate([head, rolled[8:, :]], axis=0)
        y = y + shifted * cw_ref[CONV_K - 1 - sh:CONV_K - sh, :]
    tail_ref[...] = raw[SEQ_TILE - 8:, :]
    xbc_ref[...] = (y * jax.nn.sigmoid(y)).astype(BF16)

    sb = small + bias_ref[...]
    l1p = jnp.log1p(jnp.exp(-jnp.abs(sb)))
    softplus = jnp.maximum(sb, 0.0) + l1p
    log_sig = -(jnp.maximum(-sb, 0.0) + l1p)
    dmat = jnp.where(lane < AUX_CUM, softplus * arow_ref[...],
                     jnp.where(lane < AUX_CUM + N_HEADS, log_sig * LOG2_E, 0.0))
    r = lax.broadcasted_iota(jnp.int32, (CHUNK, CHUNK), 0)
    c = lax.broadcasted_iota(jnp.int32, (CHUNK, CHUNK), 1)
    tri = (r >= c).astype(BF16)
    lane_c = lax.broadcasted_iota(jnp.int32, (CHUNK, LANES), 1)
    lane_r = lax.broadcasted_iota(jnp.int32, (1, LANES), 1)
    run = run_ref[0:1, :]
    pieces = []
    for ci in range(SEQ_TILE // CHUNK):
        rows = slice(CHUNK * ci, CHUNK * (ci + 1))
        hi, mid, lo = _split3(dmat[rows, :])
        cs = _dot(tri, hi) + _dot(tri, mid) + _dot(tri, lo)
        pieces.append(jnp.where(lane_c < AUX_ACS, softplus[rows, :], cs + run))
        run = run + jnp.where(lane_r >= AUX_CUM, cs[CHUNK - 1:CHUNK, :], 0.0)
    run_ref[...] = jnp.broadcast_to(run, run_ref.shape)
    aux = jnp.concatenate(pieces, axis=0)
    aux_ref[...] = aux
    auxt_ref[...] = aux.T[0:AUX_ROWS, :]


def _inproj(h1, nw, win, cw, cb, bias_row, a_row):
    b, s, _ = h1.shape
    tile = lambda d: pl.BlockSpec((None, SEQ_TILE, d), lambda bi, si: (bi, si, 0))
    outs = [
        jax.ShapeDtypeStruct((b, s, D_SSM), BF16),
        jax.ShapeDtypeStruct((b, s, D_CONV), BF16),
        jax.ShapeDtypeStruct((b, s, D_HEADS_PAD), BF16),
        jax.ShapeDtypeStruct((b, s, D_HEADS_PAD), BF16),
        jax.ShapeDtypeStruct((b, s, D_HEADS_PAD), BF16),
        jax.ShapeDtypeStruct((b, s, LANES), F32),
        jax.ShapeDtypeStruct((b, AUX_ROWS, s), F32),
    ]
    return pl.pallas_call(
        _inproj_kernel,
        grid=(b, s // SEQ_TILE),
        in_specs=[
            tile(D_MODEL),
            _const_spec((1, D_MODEL)),
            _const_spec((D_MODEL, D_PROJ)),
            _const_spec((CONV_K, D_CONV)),
            _const_spec((1, D_CONV)),
            _const_spec((1, LANES)),
            _const_spec((1, LANES)),
        ],
        out_specs=[tile(D_SSM), tile(D_CONV), tile(D_HEADS_PAD), tile(D_HEADS_PAD), tile(D_HEADS_PAD),
                   tile(LANES),
                   pl.BlockSpec((None, AUX_ROWS, SEQ_TILE), lambda bi, si: (bi, 0, si))],
        out_shape=outs,
        scratch_shapes=[pltpu.VMEM((8, D_CONV), F32), pltpu.VMEM((8, LANES), F32)],
        compiler_params=pltpu.CompilerParams(
            dimension_semantics=("arbitrary", "arbitrary"), vmem_limit_bytes=VMEM_LIMIT),
        name="inproj",
    )(h1, nw, win, cw, cb, bias_row, a_row)


def _ssd_kernel(xbc_ref, z_ref, aux_ref, auxt_ref, dskip_ref, nw_ref, o_ref, st_ref):
    st_ref[...] = jnp.zeros_like(st_ref)
    n_chunks = xbc_ref.shape[0] // CHUNK
    r = lax.broadcasted_iota(jnp.int32, (CHUNK, CHUNK), 0)
    c = lax.broadcasted_iota(jnp.int32, (CHUNK, CHUNK), 1)
    causal = r >= c
    lo_half = c < HEAD_DIM
    hpg = N_HEADS // SSM_GROUPS
    gw = D_SSM // SSM_GROUPS

    def chunk_body(ci, carry):
        l0 = pl.multiple_of(ci * CHUNK, CHUNK)
        aux = aux_ref[pl.ds(l0, CHUNK), :]
        auxt = auxt_ref[:, pl.ds(l0, CHUNK)]
        ys = []
        for g in range(SSM_GROUPS):
            b_g = xbc_ref[pl.ds(l0, CHUNK), D_SSM + D_STATE * g:D_SSM + D_STATE * (g + 1)]
            c_g = xbc_ref[pl.ds(l0, CHUNK),
                          D_SSM + D_STATE * (SSM_GROUPS + g):D_SSM + D_STATE * (SSM_GROUPS + g + 1)]
            cb = _dot_nt(c_g, b_g)
            b_gt = b_g.astype(F32).T.astype(BF16)
            for pp in range(hpg // 2):
                pair = g * (hpg // 2) + pp
                xs_bf = xbc_ref[pl.ds(l0, CHUNK), LANES * pair:LANES * (pair + 1)]
                xs = xs_bf.astype(F32)
                y_d, w_col, e_col, cd = [], [], [], []
                for e in range(2):
                    he = 2 * pair + e
                    acs_col = aux[:, AUX_ACS + he:AUX_ACS + he + 1]
                    dt_col = aux[:, AUX_DT + he:AUX_DT + he + 1]
                    acs_last = aux[CHUNK - 1:CHUNK, AUX_ACS + he:AUX_ACS + he + 1]
                    acs_row = auxt[AUX_ACS + he:AUX_ACS + he + 1, :]
                    dt_row = auxt[AUX_DT + he:AUX_DT + he + 1, :]
                    decay = jnp.exp(jnp.where(causal, acs_col - acs_row, -jnp.inf))
                    m = (cb * decay * dt_row).astype(BF16)
                    y_d.append(_dot(m, xs_bf))
                    w_col.append(dt_col * jnp.exp(acs_last - acs_col))
                    e_col.append(jnp.exp(acs_col))
                    cd.append(jnp.exp(acs_last))
                st = st_ref[pair]
                y_off = _dot(c_g, st.astype(BF16)) * jnp.where(lo_half, e_col[0], e_col[1])
                xs_w = (xs * jnp.where(lo_half, w_col[0], w_col[1])).astype(BF16)
                st_ref[pair] = st * jnp.where(lo_half, cd[0], cd[1]) + _dot(b_gt, xs_w)
                y = jnp.where(lo_half, y_d[0], y_d[1]) + y_off
                y = y + dskip_ref[:, LANES * pair:LANES * (pair + 1)] * xs
                ys.append(y)
        outs = []
        for g in range(SSM_GROUPS):
            yg = jnp.concatenate(ys[g * (hpg // 2):(g + 1) * (hpg // 2)], axis=1)
            zg = z_ref[pl.ds(l0, CHUNK), gw * g:gw * (g + 1)].astype(F32)
            gg = yg * (zg * jax.nn.sigmoid(zg))
            gg = gg * lax.rsqrt(jnp.mean(gg * gg, axis=-1, keepdims=True) + EPS)
            outs.append(gg * nw_ref[:, gw * g:gw * (g + 1)])
        o_ref[pl.ds(l0, CHUNK), :] = jnp.concatenate(outs, axis=1).astype(BF16)
        return carry

    lax.fori_loop(0, n_chunks, chunk_body, 0)


def _ssd(xbc, z, aux, auxt, dskip_row, nw):
    b, s, _ = xbc.shape
    full = lambda r, d: pl.BlockSpec((None, r, d), lambda bi: (bi, 0, 0))
    return pl.pallas_call(
        _ssd_kernel,
        grid=(b,),
        in_specs=[full(s, D_CONV), full(s, D_SSM), full(s, LANES), full(AUX_ROWS, s),
                  _const_spec((1, D_SSM)), _const_spec((1, D_SSM))],
        out_specs=full(s, D_SSM),
        out_shape=jax.ShapeDtypeStruct((b, s, D_SSM), BF16),
        scratch_shapes=[pltpu.VMEM((N_HEADS // 2, D_STATE, LANES), F32)],
        compiler_params=pltpu.CompilerParams(
            dimension_semantics=("arbitrary",), vmem_limit_bytes=VMEM_LIMIT),
        name="ssd",
    )(xbc, z, aux, auxt, dskip_row, nw)


def _fox_kernel(q_ref, k_ref, v_ref, auxt_ref, nw_ref, o_ref, m_ref, acc_ref):
    qi = pl.program_id(1)
    q0 = pl.multiple_of(qi * Q_TILE, Q_TILE)
    r = lax.broadcasted_iota(jnp.int32, (Q_TILE, Q_TILE), 0)
    c = lax.broadcasted_iota(jnp.int32, (Q_TILE, Q_TILE), 1)
    causal = r >= c

    def scores(he, k0):
        cols = slice(LANES * he, LANES * (he + 1))
        ck = auxt_ref[AUX_CUM + he:AUX_CUM + he + 1, pl.ds(k0, Q_TILE)]
        return _dot_nt(q_ref[:, cols], k_ref[pl.ds(k0, Q_TILE), cols]) - ck

    def probs(s, m):
        return jnp.exp2(s - jnp.concatenate([m] * (Q_TILE // LANES), axis=1)).astype(BF16)

    for he in range(N_HEADS):
        s = jnp.where(causal, scores(he, q0), -jnp.inf)
        m0 = jnp.broadcast_to(jnp.max(s, axis=-1, keepdims=True), (Q_TILE, LANES))
        m_ref[he] = m0
        acc_ref[he] = _dot(probs(s, m0), v_ref[pl.ds(q0, Q_TILE), LANES * he:LANES * (he + 1)])

    def body(j, carry):
        k0 = pl.multiple_of(j * Q_TILE, Q_TILE)
        for he in range(N_HEADS):
            s = scores(he, k0)
            m_old = m_ref[he]
            m_new = jnp.maximum(m_old, jnp.max(s, axis=-1, keepdims=True))
            m_ref[he] = m_new
            pv = _dot(probs(s, m_new), v_ref[pl.ds(k0, Q_TILE), LANES * he:LANES * (he + 1)])
            acc_ref[he] = jnp.exp2(m_old - m_new) * acc_ref[he] + pv
        return carry

    lax.fori_loop(0, qi, body, 0)
    lane = lax.broadcasted_iota(jnp.int32, (Q_TILE, LANES), 1)
    lo_half = lane < HEAD_DIM
    pair_outs = []
    for pair in range(N_HEADS // 2):
        o0, o1 = (acc_ref[2 * pair + e] / acc_ref[2 * pair + e][:, HEAD_DIM:HEAD_DIM + 1] for e in range(2))
        pair_outs.append(jnp.where(lo_half, o0, pltpu.roll(o1, HEAD_DIM, axis=1)))
    y = jnp.concatenate(pair_outs, axis=1)
    o_ref[...] = _rms(y, nw_ref[...]).astype(BF16)


def _fox(q, k, v, auxt, nw):
    b, s, dp = q.shape
    return pl.pallas_call(
        _fox_kernel,
        grid=(b, s // Q_TILE),
        in_specs=[
            pl.BlockSpec((None, Q_TILE, dp), lambda bi, qi: (bi, qi, 0)),
            pl.BlockSpec((None, s, dp), lambda bi, qi: (bi, 0, 0)),
            pl.BlockSpec((None, s, dp), lambda bi, qi: (bi, 0, 0)),
            pl.BlockSpec((None, AUX_ROWS, s), lambda bi, qi: (bi, 0, 0)),
            _const_spec((1, D_FOX)),
        ],
        out_specs=pl.BlockSpec((None, Q_TILE, D_FOX), lambda bi, qi: (bi, qi, 0)),
        out_shape=jax.ShapeDtypeStruct((b, s, D_FOX), BF16),
        scratch_shapes=[pltpu.VMEM((N_HEADS, Q_TILE, LANES), F32),
                        pltpu.VMEM((N_HEADS, Q_TILE, LANES), F32)],
        compiler_params=pltpu.CompilerParams(
            dimension_semantics=("arbitrary", "arbitrary"), vmem_limit_bytes=VMEM_LIMIT),
        name="fox",
    )(q, k, v, auxt, nw)


def _tail_kernel(h_ref, ys_ref, yf_ref, p_ref, wos_ref, wof_ref, n2_ref, wgu_ref, wd_ref,
                 pn_ref, pg_ref, pu_ref, fn_ref, o_ref, acc_ref):
    h2 = h_ref[...] + _dot(ys_ref[...], wos_ref[...]) + _dot(yf_ref[...], wof_ref[...])
    xn = _rms(h2, n2_ref[...]).astype(BF16)
    _swiglu_into(xn, wgu_ref, wd_ref, acc_ref)
    h3 = h2 + 0.5 * acc_ref[...]
    gate = jax.nn.sigmoid(_dot(_rms(h3, pn_ref[...]).astype(BF16), pg_ref[...]))
    h4 = h3 + gate * _dot(p_ref[...].astype(BF16), pu_ref[...])
    o_ref[...] = _rms(h4, fn_ref[...])


def _tail(h1, ys, yf, p2d, wos, wof, n2, wgu, wd, pn, pg, pu, fn):
    t = h1.shape[0]
    row = lambda d: pl.BlockSpec((TOKEN_TILE, d), lambda i: (i, 0))
    return pl.pallas_call(
        _tail_kernel,
        grid=(t // TOKEN_TILE,),
        in_specs=[
            row(D_MODEL), row(D_SSM), row(D_FOX), row(D_PLE),
            _const_spec((D_SSM, D_MODEL)), _const_spec((D_FOX, D_MODEL)),
            _const_spec((1, D_MODEL)),
            _const_spec((D_MODEL, 2 * D_FF)), _const_spec((D_FF, D_MODEL)),
            _const_spec((1, D_MODEL)), _const_spec((D_MODEL, D_MODEL)),
            _const_spec((D_PLE, D_MODEL)), _const_spec((1, D_MODEL)),
        ],
        out_specs=row(D_MODEL),
        out_shape=jax.ShapeDtypeStruct((t, D_MODEL), F32),
        scratch_shapes=[pltpu.VMEM((TOKEN_TILE, D_MODEL), F32)],
        compiler_params=pltpu.CompilerParams(
            dimension_semantics=("arbitrary",), vmem_limit_bytes=VMEM_LIMIT),
        name="tail",
    )(h1, ys, yf, p2d, wos, wof, n2, wgu, wd, pn, pg, pu, fn)


def _interleave_gate_up(w_gate, w_up):
    d = w_gate.shape[0]
    g = w_gate.reshape(d, N_FF_CHUNKS, 1, FF_CHUNK)
    u = w_up.reshape(d, N_FF_CHUNKS, 1, FF_CHUNK)
    return jnp.concatenate([g, u], axis=2).reshape(d, 2 * D_FF).astype(BF16)


def _pad_lanes(v):
    return jnp.pad(v, (0, LANES - v.shape[0])).reshape(1, LANES).astype(F32)


def kernel(x, p, ffn1_norm, ffn1_w_gate, ffn1_w_up, ffn1_w_down, mix_norm, w_in, conv_w, conv_b, dt_bias, a_log, d_skip, ssd_norm, fox_f_bias, fox_norm, w_out, ffn2_norm, ffn2_w_gate, ffn2_w_up, ffn2_w_down, ple_norm, ple_w_gate, ple_w_up, final_norm):
    bsz, seq, _ = x.shape
    depth = p.shape[0]
    row = lambda v: v.reshape(1, -1).astype(F32)
    h = x.reshape(bsz * seq, D_MODEL)
    for i in range(depth):
        h1 = _ffn1(h, row(ffn1_norm[i]), _interleave_gate_up(ffn1_w_gate[i], ffn1_w_up[i]),
                   ffn1_w_down[i].astype(BF16))

        w = w_in[i]
        o_z, o_xbc, o_dt = D_SSM, D_SSM + D_CONV, D_SSM + D_CONV + N_HEADS
        o_q, o_k, o_v = o_dt + D_FOX, o_dt + 2 * D_FOX, o_dt + 3 * D_FOX
        w_dt, w_f = w[:, o_xbc:o_dt], w[:, o_v:]
        gates = jnp.concatenate(
            [w_dt, w_dt, w_f, jnp.zeros((D_MODEL, LANES - 3 * N_HEADS), w.dtype)], axis=1)
        win = jnp.concatenate(
            [w[:, :o_xbc], w[:, o_dt:o_q] * (HEAD_DIM ** -0.5 * LOG2_E), w[:, o_q:o_v], gates],
            axis=1).astype(BF16)
        bias_row = _pad_lanes(jnp.concatenate([dt_bias[i], dt_bias[i], fox_f_bias[i]]))
        a_neg = -jnp.exp(a_log[i].astype(F32))
        a_row = _pad_lanes(jnp.concatenate([jnp.zeros_like(a_neg), a_neg]))
        z, xbc, q, k, v, aux, auxt = _inproj(
            h1.reshape(bsz, seq, D_MODEL), row(mix_norm[i]), win,
            conv_w[i].astype(F32), row(conv_b[i]), bias_row, a_row)

        y_ssd = _ssd(xbc, z, aux, auxt, row(jnp.repeat(d_skip[i], HEAD_DIM)), row(ssd_norm[i]))
        y_fox = _fox(q, k, v, auxt, row(fox_norm[i]))

        wo = w_out[i].astype(BF16)
        assert depth == 1
        h = _tail(h1, y_ssd.reshape(bsz * seq, D_SSM), y_fox.reshape(bsz * seq, D_FOX),
                  p[i].reshape(bsz * seq, D_PLE), wo[:D_SSM], wo[D_SSM:], row(ffn2_norm[i]),
                  _interleave_gate_up(ffn2_w_gate[i], ffn2_w_up[i]), ffn2_w_down[i].astype(BF16),
                  row(ple_norm[i]), ple_w_gate[i].astype(BF16), ple_w_up[i].astype(BF16),
                  row(final_norm))
    return h.reshape(bsz, seq, D_MODEL)
```

```python
import jax
import jax.numpy as jnp
from jax import lax
from jax.experimental import pallas as pl
from jax.experimental.pallas import tpu as pltpu

F32 = jnp.float32
BF16 = jnp.bfloat16

D_MODEL = 1024
D_SSM = 512
HEAD_DIM = 64
N_HEADS = 8
SSM_GROUPS = 2
D_STATE = 128
CONV_K = 4
CHUNK = 128
D_CONV = D_SSM + 2 * SSM_GROUPS * D_STATE
D_FOX = 512
D_HEADS_PAD = 1024
D_FF = 2816
D_PLE = 256
EPS = 1e-6
LOG2_E = 1.4426950408889634

LANES = 128
FF_CHUNK = 256
N_FF_CHUNKS = D_FF // FF_CHUNK
TOKEN_TILE = 512
SEQ_TILE = 512
Q_TILE = 512
K_TILE = 256
D_PROJ = D_SSM + D_CONV + 3 * D_FOX + LANES
VMEM_LIMIT = 56 * 1024 * 1024

AUX_DT = 0
AUX_ACS = 8
AUX_CUM = 16
AUX_ROWS = 32


def _rms(x, w):
    ms = jnp.mean(x * x, axis=-1, keepdims=True)
    return x * lax.rsqrt(ms + EPS) * w


def _dot(a, b):
    return jnp.dot(a, b, preferred_element_type=F32)


def _dot_nt(a, b):
    return lax.dot_general(a, b, (((1,), (1,)), ((), ())), preferred_element_type=F32)


def _swiglu_into(xn, wg_ref, wu_ref, wd_ref, acc_ref):
    for j in range(N_FF_CHUNKS):
        g = _dot(xn, wg_ref[:, FF_CHUNK * j:FF_CHUNK * (j + 1)])
        u = _dot(xn, wu_ref[:, FF_CHUNK * j:FF_CHUNK * (j + 1)])
        a = (g * jax.nn.sigmoid(g) * u).astype(BF16)
        d = _dot(a, wd_ref[FF_CHUNK * j:FF_CHUNK * (j + 1), :])
        if j == 0:
            acc_ref[...] = d
        else:
            acc_ref[...] += d


def _ffn1_kernel(x_ref, nw_ref, wg_ref, wu_ref, wd_ref, o_ref, acc_ref):
    x = x_ref[...]
    xn = _rms(x, nw_ref[...]).astype(BF16)
    _swiglu_into(xn, wg_ref, wu_ref, wd_ref, acc_ref)
    o_ref[...] = x + 0.5 * acc_ref[...]


def _const_spec(shape):
    nd = len(shape)
    return pl.BlockSpec(shape, lambda *_: (0,) * nd, pipeline_mode=pl.Buffered(1))


def _ffn1(x2d, nw, wg, wu, wd):
    t = x2d.shape[0]
    return pl.pallas_call(
        _ffn1_kernel,
        grid=(t // TOKEN_TILE,),
        in_specs=[
            pl.BlockSpec((TOKEN_TILE, D_MODEL), lambda i: (i, 0)),
            _const_spec((1, D_MODEL)),
            _const_spec((D_MODEL, D_FF)),
            _const_spec((D_MODEL, D_FF)),
            _const_spec((D_FF, D_MODEL)),
        ],
        out_specs=pl.BlockSpec((TOKEN_TILE, D_MODEL), lambda i: (i, 0)),
        out_shape=jax.ShapeDtypeStruct((t, D_MODEL), F32),
        scratch_shapes=[pltpu.VMEM((TOKEN_TILE, D_MODEL), F32)],
        compiler_params=pltpu.CompilerParams(
            dimension_semantics=("arbitrary",), vmem_limit_bytes=VMEM_LIMIT),
        name="ffn1",
    )(x2d, nw, wg, wu, wd)


def _split3(x):
    hi = x.astype(BF16)
    r1 = x - hi.astype(F32)
    mid = r1.astype(BF16)
    lo = (r1 - mid.astype(F32)).astype(BF16)
    return hi, mid, lo


def _inproj_kernel(h_ref, nw_ref, win_ref, cw_ref, cb_ref, bias_ref, arow_ref,
                   z_ref, xbc_ref, q_ref, k_ref, v_ref, aux_ref, auxt_ref,
                   xpad_ref, run_ref):
    s_idx = pl.program_id(1)

    @pl.when(s_idx == 0)
    def _():
        xpad_ref[0:8, :] = jnp.zeros((8, D_CONV), F32)
        run_ref[...] = jnp.zeros_like(run_ref)

    u = _rms(h_ref[...], nw_ref[...]).astype(BF16)
    c0 = 0
    z_ref[...] = _dot(u, win_ref[:, c0:c0 + D_SSM]).astype(BF16)
    c0 += D_SSM
    raw = _dot(u, win_ref[:, c0:c0 + D_CONV])
    c0 += D_CONV
    lane = lax.broadcasted_iota(jnp.int32, (SEQ_TILE, LANES), 1)
    lo_half = lane < HEAD_DIM
    for dst_ref, fill in ((q_ref, 0.0), (k_ref, 0.0), (v_ref, jnp.where(lane == HEAD_DIM, 1.0, 0.0))):
        full = _dot(u, win_ref[:, c0:c0 + D_FOX])
        c0 += D_FOX
        for pair in range(N_HEADS // 2):
            blk = full[:, LANES * pair:LANES * (pair + 1)]
            swapped = pltpu.roll(blk, HEAD_DIM, axis=1)
            dst_ref[:, LANES * 2 * pair:LANES * (2 * pair + 1)] = jnp.where(lo_half, blk, fill).astype(BF16)
            dst_ref[:, LANES * (2 * pair + 1):LANES * (2 * pair + 2)] = jnp.where(lo_half, swapped, fill).astype(BF16)
    small = _dot(u, win_ref[:, c0:c0 + LANES])

    xpad_ref[8:, :] = raw
    y = raw * cw_ref[CONV_K - 1:CONV_K, :] + cb_ref[...]
    for sh in range(1, CONV_K):
        y = y + xpad_ref[8 - sh:8 - sh + SEQ_TILE, :] * cw_ref[CONV_K - 1 - sh:CONV_K - sh, :]
    xpad_ref[0:8, :] = raw[SEQ_TILE - 8:, :]
    xbc_ref[...] = (y * jax.nn.sigmoid(y)).astype(BF16)

    sb = small + bias_ref[...]
    l1p = jnp.log1p(jnp.exp(-jnp.abs(sb)))
    softplus = jnp.maximum(sb, 0.0) + l1p
    log_sig = -(jnp.maximum(-sb, 0.0) + l1p)
    dmat = jnp.where(lane < AUX_CUM, softplus * arow_ref[...],
                     jnp.where(lane < AUX_CUM + N_HEADS, log_sig * LOG2_E, 0.0))
    r = lax.broadcasted_iota(jnp.int32, (CHUNK, CHUNK), 0)
    c = lax.broadcasted_iota(jnp.int32, (CHUNK, CHUNK), 1)
    tri = (r >= c).astype(BF16)
    lane_c = lax.broadcasted_iota(jnp.int32, (CHUNK, LANES), 1)
    lane_r = lax.broadcasted_iota(jnp.int32, (1, LANES), 1)
    run = run_ref[0:1, :]
    pieces = []
    for ci in range(SEQ_TILE // CHUNK):
        rows = slice(CHUNK * ci, CHUNK * (ci + 1))
        hi, mid, lo = _split3(dmat[rows, :])
        cs = _dot(tri, hi) + _dot(tri, mid) + _dot(tri, lo)
        pieces.append(jnp.where(lane_c < AUX_ACS, softplus[rows, :], cs + run))
        run = run + jnp.where(lane_r >= AUX_CUM, cs[CHUNK - 1:CHUNK, :], 0.0)
    run_ref[...] = jnp.broadcast_to(run, run_ref.shape)
    aux = jnp.concatenate(pieces, axis=0)
    aux_ref[...] = aux
    auxt_ref[...] = aux.T[0:AUX_ROWS, :]


def _inproj(h1, nw, win, cw, cb, bias_row, a_row):
    b, s, _ = h1.shape
    tile = lambda d: pl.BlockSpec((None, SEQ_TILE, d), lambda bi, si: (bi, si, 0))
    outs = [
        jax.ShapeDtypeStruct((b, s, D_SSM), BF16),
        jax.ShapeDtypeStruct((b, s, D_CONV), BF16),
        jax.ShapeDtypeStruct((b, s, D_HEADS_PAD), BF16),
        jax.ShapeDtypeStruct((b, s, D_HEADS_PAD), BF16),
        jax.ShapeDtypeStruct((b, s, D_HEADS_PAD), BF16),
        jax.ShapeDtypeStruct((b, s, LANES), F32),
        jax.ShapeDtypeStruct((b, AUX_ROWS, s), F32),
    ]
    return pl.pallas_call(
        _inproj_kernel,
        grid=(b, s // SEQ_TILE),
        in_specs=[
            tile(D_MODEL),
            _const_spec((1, D_MODEL)),
            _const_spec((D_MODEL, D_PROJ)),
            _const_spec((CONV_K, D_CONV)),
            _const_spec((1, D_CONV)),
            _const_spec((1, LANES)),
            _const_spec((1, LANES)),
        ],
        out_specs=[tile(D_SSM), tile(D_CONV), tile(D_HEADS_PAD), tile(D_HEADS_PAD), tile(D_HEADS_PAD),
                   tile(LANES),
                   pl.BlockSpec((None, AUX_ROWS, SEQ_TILE), lambda bi, si: (bi, 0, si))],
        out_shape=outs,
        scratch_shapes=[pltpu.VMEM((SEQ_TILE + 8, D_CONV), F32), pltpu.VMEM((8, LANES), F32)],
        compiler_params=pltpu.CompilerParams(
            dimension_semantics=("arbitrary", "arbitrary"), vmem_limit_bytes=VMEM_LIMIT),
        name="inproj",
    )(h1, nw, win, cw, cb, bias_row, a_row)


def _ssd_kernel(xbc_ref, z_ref, aux_ref, auxt_ref, dskip_ref, nw_ref, o_ref, st_ref):
    st_ref[...] = jnp.zeros_like(st_ref)
    n_chunks = xbc_ref.shape[0] // CHUNK
    r = lax.broadcasted_iota(jnp.int32, (CHUNK, CHUNK), 0)
    c = lax.broadcasted_iota(jnp.int32, (CHUNK, CHUNK), 1)
    causal = r >= c
    lo_half = c < HEAD_DIM
    hpg = N_HEADS // SSM_GROUPS
    gw = D_SSM // SSM_GROUPS

    def chunk_body(ci, carry):
        l0 = pl.multiple_of(ci * CHUNK, CHUNK)
        aux = aux_ref[pl.ds(l0, CHUNK), :]
        auxt = auxt_ref[:, pl.ds(l0, CHUNK)]
        ys = []
        for g in range(SSM_GROUPS):
            b_g = xbc_ref[pl.ds(l0, CHUNK), D_SSM + D_STATE * g:D_SSM + D_STATE * (g + 1)]
            c_g = xbc_ref[pl.ds(l0, CHUNK),
                          D_SSM + D_STATE * (SSM_GROUPS + g):D_SSM + D_STATE * (SSM_GROUPS + g + 1)]
            cb = _dot_nt(c_g, b_g)
            b_gt = b_g.astype(F32).T.astype(BF16)
            for pp in range(hpg // 2):
                pair = g * (hpg // 2) + pp
                xs_bf = xbc_ref[pl.ds(l0, CHUNK), LANES * pair:LANES * (pair + 1)]
                xs = xs_bf.astype(F32)
                y_d, w_col, e_col, cd = [], [], [], []
                for e in range(2):
                    he = 2 * pair + e
                    acs_col = aux[:, AUX_ACS + he:AUX_ACS + he + 1]
                    dt_col = aux[:, AUX_DT + he:AUX_DT + he + 1]
                    acs_last = aux[CHUNK - 1:CHUNK, AUX_ACS + he:AUX_ACS + he + 1]
                    acs_row = auxt[AUX_ACS + he:AUX_ACS + he + 1, :]
                    dt_row = auxt[AUX_DT + he:AUX_DT + he + 1, :]
                    decay = jnp.exp(jnp.where(causal, acs_col - acs_row, -jnp.inf))
                    m = (cb * decay * dt_row).astype(BF16)
                    y_d.append(_dot(m, xs_bf))
                    w_col.append(dt_col * jnp.exp(acs_last - acs_col))
                    e_col.append(jnp.exp(acs_col))
                    cd.append(jnp.exp(acs_last))
                st = st_ref[pair]
                y_off = _dot(c_g, st.astype(BF16)) * jnp.where(lo_half, e_col[0], e_col[1])
                xs_w = (xs * jnp.where(lo_half, w_col[0], w_col[1])).astype(BF16)
                st_ref[pair] = st * jnp.where(lo_half, cd[0], cd[1]) + _dot(b_gt, xs_w)
                y = jnp.where(lo_half, y_d[0], y_d[1]) + y_off
                y = y + dskip_ref[:, LANES * pair:LANES * (pair + 1)] * xs
                ys.append(y)
        outs = []
        for g in range(SSM_GROUPS):
            yg = jnp.concatenate(ys[g * (hpg // 2):(g + 1) * (hpg // 2)], axis=1)
            zg = z_ref[pl.ds(l0, CHUNK), gw * g:gw * (g + 1)].astype(F32)
            gg = yg * (zg * jax.nn.sigmoid(zg))
            gg = gg * lax.rsqrt(jnp.mean(gg * gg, axis=-1, keepdims=True) + EPS)
            outs.append(gg * nw_ref[:, gw * g:gw * (g + 1)])
        o_ref[pl.ds(l0, CHUNK), :] = jnp.concatenate(outs, axis=1).astype(BF16)
        return carry

    lax.fori_loop(0, n_chunks, chunk_body, 0)


def _ssd(xbc, z, aux, auxt, dskip_row, nw):
    b, s, _ = xbc.shape
    full = lambda r, d: pl.BlockSpec((None, r, d), lambda bi: (bi, 0, 0))
    return pl.pallas_call(
        _ssd_kernel,
        grid=(b,),
        in_specs=[full(s, D_CONV), full(s, D_SSM), full(s, LANES), full(AUX_ROWS, s),
                  _const_spec((1, D_SSM)), _const_spec((1, D_SSM))],
        out_specs=full(s, D_SSM),
        out_shape=jax.ShapeDtypeStruct((b, s, D_SSM), BF16),
        scratch_shapes=[pltpu.VMEM((N_HEADS // 2, D_STATE, LANES), F32)],
        compiler_params=pltpu.CompilerParams(
            dimension_semantics=("arbitrary",), vmem_limit_bytes=VMEM_LIMIT),
        name="ssd",
    )(xbc, z, aux, auxt, dskip_row, nw)


def _fox_kernel(q_ref, k_ref, v_ref, auxt_ref, nw_ref, o_ref, m_ref, acc_ref):
    qi = pl.program_id(1)
    q0 = pl.multiple_of(qi * Q_TILE, Q_TILE)
    n_sub = Q_TILE // K_TILE

    def scores(he, k0, rows):
        cols = slice(LANES * he, LANES * (he + 1))
        ck = auxt_ref[AUX_CUM + he:AUX_CUM + he + 1, pl.ds(k0, K_TILE)]
        return _dot_nt(q_ref[rows, cols], k_ref[pl.ds(k0, K_TILE), cols]) - ck

    def update(he, s, k0, rows, first):
        bmax = jnp.max(s, axis=-1, keepdims=True)
        if first:
            m_new = jnp.broadcast_to(bmax, (s.shape[0], LANES))
        else:
            m_old = m_ref[he, rows, :]
            m_new = jnp.maximum(m_old, bmax)
        m_ref[he, rows, :] = m_new
        p = jnp.exp2(s - jnp.concatenate([m_new] * (K_TILE // LANES), axis=1)).astype(BF16)
        pv = _dot(p, v_ref[pl.ds(k0, K_TILE), LANES * he:LANES * (he + 1)])
        if first:
            acc_ref[he, rows, :] = pv
        else:
            acc_ref[he, rows, :] = jnp.exp2(m_old - m_new) * acc_ref[he, rows, :] + pv

    for d in range(n_sub):
        rows = slice(K_TILE * d, Q_TILE)
        nrows = Q_TILE - K_TILE * d
        causal = (lax.broadcasted_iota(jnp.int32, (nrows, K_TILE), 0)
                  >= lax.broadcasted_iota(jnp.int32, (nrows, K_TILE), 1))
        for he in range(N_HEADS):
            k0 = pl.multiple_of(q0 + K_TILE * d, K_TILE)
            s = jnp.where(causal, scores(he, k0, rows), -jnp.inf)
            update(he, s, k0, rows, first=(d == 0))

    def body(j, carry):
        k0 = pl.multiple_of(j * K_TILE, K_TILE)
        for he in range(N_HEADS):
            update(he, scores(he, k0, slice(None)), k0, slice(None), first=False)
        return carry

    lax.fori_loop(0, qi * n_sub, body, 0)
    lane = lax.broadcasted_iota(jnp.int32, (Q_TILE, LANES), 1)
    lo_half = lane < HEAD_DIM
    pair_outs = []
    for pair in range(N_HEADS // 2):
        o0, o1 = (acc_ref[2 * pair + e] / acc_ref[2 * pair + e][:, HEAD_DIM:HEAD_DIM + 1] for e in range(2))
        pair_outs.append(jnp.where(lo_half, o0, pltpu.roll(o1, HEAD_DIM, axis=1)))
    y = jnp.concatenate(pair_outs, axis=1)
    o_ref[...] = _rms(y, nw_ref[...]).astype(BF16)


def _fox(q, k, v, auxt, nw):
    b, s, dp = q.shape
    return pl.pallas_call(
        _fox_kernel,
        grid=(b, s // Q_TILE),
        in_specs=[
            pl.BlockSpec((None, Q_TILE, dp), lambda bi, qi: (bi, qi, 0)),
            pl.BlockSpec((None, s, dp), lambda bi, qi: (bi, 0, 0)),
            pl.BlockSpec((None, s, dp), lambda bi, qi: (bi, 0, 0)),
            pl.BlockSpec((None, AUX_ROWS, s), lambda bi, qi: (bi, 0, 0)),
            _const_spec((1, D_FOX)),
        ],
        out_specs=pl.BlockSpec((None, Q_TILE, D_FOX), lambda bi, qi: (bi, qi, 0)),
        out_shape=jax.ShapeDtypeStruct((b, s, D_FOX), BF16),
        scratch_shapes=[pltpu.VMEM((N_HEADS, Q_TILE, LANES), F32),
                        pltpu.VMEM((N_HEADS, Q_TILE, LANES), F32)],
        compiler_params=pltpu.CompilerParams(
            dimension_semantics=("arbitrary", "arbitrary"), vmem_limit_bytes=VMEM_LIMIT),
        name="fox",
    )(q, k, v, auxt, nw)


def _tail_kernel(h_ref, ys_ref, yf_ref, p_ref, wos_ref, wof_ref, n2_ref, wg_ref, wu_ref, wd_ref,
                 pn_ref, pg_ref, pu_ref, fn_ref, o_ref, acc_ref):
    h2 = h_ref[...] + _dot(ys_ref[...], wos_ref[...]) + _dot(yf_ref[...], wof_ref[...])
    xn = _rms(h2, n2_ref[...]).astype(BF16)
    _swiglu_into(xn, wg_ref, wu_ref, wd_ref, acc_ref)
    h3 = h2 + 0.5 * acc_ref[...]
    gate = jax.nn.sigmoid(_dot(_rms(h3, pn_ref[...]).astype(BF16), pg_ref[...]))
    h4 = h3 + gate * _dot(p_ref[...].astype(BF16), pu_ref[...])
    o_ref[...] = _rms(h4, fn_ref[...])


def _tail(h1, ys, yf, p2d, wos, wof, n2, wg, wu, wd, pn, pg, pu, fn):
    t = h1.shape[0]
    row = lambda d: pl.BlockSpec((TOKEN_TILE, d), lambda i: (i, 0))
    return pl.pallas_call(
        _tail_kernel,
        grid=(t // TOKEN_TILE,),
        in_specs=[
            row(D_MODEL), row(D_SSM), row(D_FOX), row(D_PLE),
            _const_spec((D_SSM, D_MODEL)), _const_spec((D_FOX, D_MODEL)),
            _const_spec((1, D_MODEL)),
            _const_spec((D_MODEL, D_FF)), _const_spec((D_MODEL, D_FF)), _const_spec((D_FF, D_MODEL)),
            _const_spec((1, D_MODEL)), _const_spec((D_MODEL, D_MODEL)),
            _const_spec((D_PLE, D_MODEL)), _const_spec((1, D_MODEL)),
        ],
        out_specs=row(D_MODEL),
        out_shape=jax.ShapeDtypeStruct((t, D_MODEL), F32),
        scratch_shapes=[pltpu.VMEM((TOKEN_TILE, D_MODEL), F32)],
        compiler_params=pltpu.CompilerParams(
            dimension_semantics=("arbitrary",), vmem_limit_bytes=VMEM_LIMIT),
        name="tail",
    )(h1, ys, yf, p2d, wos, wof, n2, wg, wu, wd, pn, pg, pu, fn)


def _pad_lanes(v):
    return jnp.pad(v, (0, LANES - v.shape[0])).reshape(1, LANES).astype(F32)


def kernel(x, p, ffn1_norm, ffn1_w_gate, ffn1_w_up, ffn1_w_down, mix_norm, w_in, conv_w, conv_b, dt_bias, a_log, d_skip, ssd_norm, fox_f_bias, fox_norm, w_out, ffn2_norm, ffn2_w_gate, ffn2_w_up, ffn2_w_down, ple_norm, ple_w_gate, ple_w_up, final_norm):
    bsz, seq, _ = x.shape
    depth = p.shape[0]
    row = lambda v: v.reshape(1, -1).astype(F32)
    h = x.reshape(bsz * seq, D_MODEL)
    for i in range(depth):
        h1 = _ffn1(h, row(ffn1_norm[i]), ffn1_w_gate[i].astype(BF16), ffn1_w_up[i].astype(BF16),
                   ffn1_w_down[i].astype(BF16))

        w = w_in[i]
        o_xbc, o_dt = D_SSM + D_CONV, D_SSM + D_CONV + N_HEADS
        o_q, o_v = o_dt + D_FOX, o_dt + 3 * D_FOX
        w_dt, w_f = w[:, o_xbc:o_dt], w[:, o_v:]
        gates = jnp.concatenate(
            [w_dt, w_dt, w_f, jnp.zeros((D_MODEL, LANES - 3 * N_HEADS), w.dtype)], axis=1)
        win = jnp.concatenate(
            [w[:, :o_xbc], w[:, o_dt:o_q] * (HEAD_DIM ** -0.5 * LOG2_E), w[:, o_q:o_v], gates],
            axis=1).astype(BF16)
        bias_row = _pad_lanes(jnp.concatenate([dt_bias[i], dt_bias[i], fox_f_bias[i]]))
        a_neg = -jnp.exp(a_log[i].astype(F32))
        a_row = _pad_lanes(jnp.concatenate([jnp.zeros_like(a_neg), a_neg]))
        z, xbc, q, k, v, aux, auxt = _inproj(
            h1.reshape(bsz, seq, D_MODEL), row(mix_norm[i]), win,
            conv_w[i].astype(F32), row(conv_b[i]), bias_row, a_row)

        y_ssd = _ssd(xbc, z, aux, auxt, row(jnp.repeat(d_skip[i], HEAD_DIM)), row(ssd_norm[i]))
        y_fox = _fox(q, k, v, auxt, row(fox_norm[i]))

        wo = w_out[i].astype(BF16)
        assert depth == 1
        h = _tail(h1, y_ssd.reshape(bsz * seq, D_SSM), y_fox.reshape(bsz * seq, D_FOX),
                  p[i].reshape(bsz * seq, D_PLE), wo[:D_SSM], wo[D_SSM:], row(ffn2_norm[i]),
                  ffn2_w_gate[i].astype(BF16), ffn2_w_up[i].astype(BF16), ffn2_w_down[i].astype(BF16),
                  row(ple_norm[i]), ple_w_gate[i].astype(BF16), ple_w_up[i].astype(BF16),
                  row(final_norm))
    return h.reshape(bsz, seq, D_MODEL)
```

```python
import functools

import jax
import jax.numpy as jnp
from jax import lax
from jax.experimental import pallas as pl
from jax.experimental.pallas import tpu as pltpu

F32 = jnp.float32
BF16 = jnp.bfloat16

D_MODEL = 1024
D_SSM = 512
HEAD_DIM = 64
N_HEADS = 8
SSM_GROUPS = 2
D_STATE = 128
CONV_K = 4
CHUNK = 128
D_CONV = D_SSM + 2 * SSM_GROUPS * D_STATE
D_FOX = 512
D_HEADS_PAD = 1024
D_FF = 2816
D_PLE = 256
EPS = 1e-6
LOG2_E = 1.4426950408889634

LANES = 128
FF_CHUNK = 256
N_FF_CHUNKS = D_FF // FF_CHUNK
TOKEN_TILE = 512
SEQ_TILE = 512
Q_TILE = 1024
K_TILE = 256
D_PROJ = D_SSM + D_CONV + 3 * D_FOX + LANES
VMEM_LIMIT = 56 * 1024 * 1024

AUX_DT = 0
AUX_ACS = 8
AUX_CUM = 16
AUX_ROWS = 32


def _rms(x, w):
    ms = jnp.mean(x * x, axis=-1, keepdims=True)
    return x * lax.rsqrt(ms + EPS) * w


def _dot(a, b):
    return jnp.dot(a, b, preferred_element_type=F32)


def _dot_nt(a, b):
    return lax.dot_general(a, b, (((1,), (1,)), ((), ())), preferred_element_type=F32)


def _swiglu_into(xn, wg_ref, wu_ref, wd_ref, acc_ref):
    for j in range(N_FF_CHUNKS):
        g = _dot(xn, wg_ref[:, FF_CHUNK * j:FF_CHUNK * (j + 1)])
        u = _dot(xn, wu_ref[:, FF_CHUNK * j:FF_CHUNK * (j + 1)])
        a = (g * jax.nn.sigmoid(g) * u).astype(BF16)
        d = _dot(a, wd_ref[FF_CHUNK * j:FF_CHUNK * (j + 1), :])
        if j == 0:
            acc_ref[...] = d
        else:
            acc_ref[...] += d


def _ffn1_kernel(x_ref, nw_ref, wg_ref, wu_ref, wd_ref, o_ref, acc_ref):
    x = x_ref[...]
    xn = _rms(x, nw_ref[...]).astype(BF16)
    _swiglu_into(xn, wg_ref, wu_ref, wd_ref, acc_ref)
    o_ref[...] = x + 0.5 * acc_ref[...]


def _const_spec(shape):
    nd = len(shape)
    return pl.BlockSpec(shape, lambda *_: (0,) * nd, pipeline_mode=pl.Buffered(1))


def _ffn1(x2d, nw, wg, wu, wd):
    t = x2d.shape[0]
    return pl.pallas_call(
        _ffn1_kernel,
        grid=(t // TOKEN_TILE,),
        in_specs=[
            pl.BlockSpec((TOKEN_TILE, D_MODEL), lambda i: (i, 0)),
            _const_spec((1, D_MODEL)),
            _const_spec((D_MODEL, D_FF)),
            _const_spec((D_MODEL, D_FF)),
            _const_spec((D_FF, D_MODEL)),
        ],
        out_specs=pl.BlockSpec((TOKEN_TILE, D_MODEL), lambda i: (i, 0)),
        out_shape=jax.ShapeDtypeStruct((t, D_MODEL), F32),
        scratch_shapes=[pltpu.VMEM((TOKEN_TILE, D_MODEL), F32)],
        compiler_params=pltpu.CompilerParams(
            dimension_semantics=("arbitrary",), vmem_limit_bytes=VMEM_LIMIT),
        name="ffn1",
    )(x2d, nw, wg, wu, wd)


def _split3(x):
    hi = x.astype(BF16)
    r1 = x - hi.astype(F32)
    mid = r1.astype(BF16)
    lo = (r1 - mid.astype(F32)).astype(BF16)
    return hi, mid, lo


def _inproj_kernel(h_ref, nw_ref, win_ref, cw_ref, cb_ref, bias_ref, arow_ref,
                   z_ref, xbc_ref, q_ref, k_ref, v_ref, aux_ref, auxt_ref,
                   xpad_ref, run_ref):
    s_idx = pl.program_id(1)

    @pl.when(s_idx == 0)
    def _():
        xpad_ref[0:8, :] = jnp.zeros((8, D_CONV), F32)
        run_ref[...] = jnp.zeros_like(run_ref)

    u = _rms(h_ref[...], nw_ref[...]).astype(BF16)
    c0 = 0
    z_ref[...] = _dot(u, win_ref[:, c0:c0 + D_SSM]).astype(BF16)
    c0 += D_SSM
    raw = _dot(u, win_ref[:, c0:c0 + D_CONV])
    c0 += D_CONV
    lane = lax.broadcasted_iota(jnp.int32, (SEQ_TILE, LANES), 1)
    lo_half = lane < HEAD_DIM
    for dst_ref, fill in ((q_ref, 0.0), (k_ref, 0.0), (v_ref, jnp.where(lane == HEAD_DIM, 1.0, 0.0))):
        full = _dot(u, win_ref[:, c0:c0 + D_FOX])
        c0 += D_FOX
        for pair in range(N_HEADS // 2):
            blk = full[:, LANES * pair:LANES * (pair + 1)]
            swapped = pltpu.roll(blk, HEAD_DIM, axis=1)
            dst_ref[:, LANES * 2 * pair:LANES * (2 * pair + 1)] = jnp.where(lo_half, blk, fill).astype(BF16)
            dst_ref[:, LANES * (2 * pair + 1):LANES * (2 * pair + 2)] = jnp.where(lo_half, swapped, fill).astype(BF16)
    small = _dot(u, win_ref[:, c0:c0 + LANES])

    xpad_ref[8:, :] = raw
    y = raw * cw_ref[CONV_K - 1:CONV_K, :] + cb_ref[...]
    for sh in range(1, CONV_K):
        y = y + xpad_ref[8 - sh:8 - sh + SEQ_TILE, :] * cw_ref[CONV_K - 1 - sh:CONV_K - sh, :]
    xpad_ref[0:8, :] = raw[SEQ_TILE - 8:, :]
    xbc_ref[...] = (y * jax.nn.sigmoid(y)).astype(BF16)

    sb = small + bias_ref[...]
    l1p = jnp.log1p(jnp.exp(-jnp.abs(sb)))
    softplus = jnp.maximum(sb, 0.0) + l1p
    log_sig = -(jnp.maximum(-sb, 0.0) + l1p)
    dmat = jnp.where(lane < AUX_CUM, softplus * arow_ref[...],
                     jnp.where(lane < AUX_CUM + N_HEADS, log_sig * LOG2_E, 0.0))
    r = lax.broadcasted_iota(jnp.int32, (CHUNK, CHUNK), 0)
    c = lax.broadcasted_iota(jnp.int32, (CHUNK, CHUNK), 1)
    tri = (r >= c).astype(BF16)
    lane_c = lax.broadcasted_iota(jnp.int32, (CHUNK, LANES), 1)
    lane_r = lax.broadcasted_iota(jnp.int32, (1, LANES), 1)
    run = run_ref[0:1, :]
    pieces = []
    for ci in range(SEQ_TILE // CHUNK):
        rows = slice(CHUNK * ci, CHUNK * (ci + 1))
        hi, mid, lo = _split3(dmat[rows, :])
        cs = _dot(tri, hi) + _dot(tri, mid) + _dot(tri, lo)
        pieces.append(jnp.where(lane_c < AUX_ACS, softplus[rows, :], cs + run))
        run = run + jnp.where(lane_r >= AUX_CUM, cs[CHUNK - 1:CHUNK, :], 0.0)
    run_ref[...] = jnp.broadcast_to(run, run_ref.shape)
    aux = jnp.concatenate(pieces, axis=0)
    aux_ref[...] = aux
    auxt_ref[...] = aux.T[0:AUX_ROWS, :]


def _inproj(h1, nw, win, cw, cb, bias_row, a_row):
    b, s, _ = h1.shape
    tile = lambda d: pl.BlockSpec((None, SEQ_TILE, d), lambda bi, si: (bi, si, 0))
    outs = [
        jax.ShapeDtypeStruct((b, s, D_SSM), BF16),
        jax.ShapeDtypeStruct((b, s, D_CONV), BF16),
        jax.ShapeDtypeStruct((b, s, D_HEADS_PAD), BF16),
        jax.ShapeDtypeStruct((b, s, D_HEADS_PAD), BF16),
        jax.ShapeDtypeStruct((b, s, D_HEADS_PAD), BF16),
        jax.ShapeDtypeStruct((b, s, LANES), F32),
        jax.ShapeDtypeStruct((b, AUX_ROWS, s), F32),
    ]
    return pl.pallas_call(
        _inproj_kernel,
        grid=(b, s // SEQ_TILE),
        in_specs=[
            tile(D_MODEL),
            _const_spec((1, D_MODEL)),
            _const_spec((D_MODEL, D_PROJ)),
            _const_spec((CONV_K, D_CONV)),
            _const_spec((1, D_CONV)),
            _const_spec((1, LANES)),
            _const_spec((1, LANES)),
        ],
        out_specs=[tile(D_SSM), tile(D_CONV), tile(D_HEADS_PAD), tile(D_HEADS_PAD), tile(D_HEADS_PAD),
                   tile(LANES),
                   pl.BlockSpec((None, AUX_ROWS, SEQ_TILE), lambda bi, si: (bi, 0, si))],
        out_shape=outs,
        scratch_shapes=[pltpu.VMEM((SEQ_TILE + 8, D_CONV), F32), pltpu.VMEM((8, LANES), F32)],
        compiler_params=pltpu.CompilerParams(
            dimension_semantics=("arbitrary", "arbitrary"), vmem_limit_bytes=VMEM_LIMIT),
        name="inproj",
    )(h1, nw, win, cw, cb, bias_row, a_row)


def _ssd_chunk(xbc_ref, z_ref, aux_ref, auxt_ref, dskip_ref, nw_ref, st_ref, l0):
    rows = slice(l0, l0 + CHUNK)
    r = lax.broadcasted_iota(jnp.int32, (CHUNK, CHUNK), 0)
    c = lax.broadcasted_iota(jnp.int32, (CHUNK, CHUNK), 1)
    causal = r >= c
    lo_half = c < HEAD_DIM
    hpg = N_HEADS // SSM_GROUPS
    gw = D_SSM // SSM_GROUPS
    aux = aux_ref[rows, :]
    auxt = auxt_ref[:, rows]
    outs = []
    for g in range(SSM_GROUPS):
        b_g = xbc_ref[rows, D_SSM + D_STATE * g:D_SSM + D_STATE * (g + 1)]
        c_g = xbc_ref[rows, D_SSM + D_STATE * (SSM_GROUPS + g):D_SSM + D_STATE * (SSM_GROUPS + g + 1)]
        cb = _dot_nt(c_g, b_g)
        b_gt = b_g.astype(F32).T
        ys = []
        for pp in range(hpg // 2):
            pair = g * (hpg // 2) + pp
            xs_bf = xbc_ref[rows, LANES * pair:LANES * (pair + 1)]
            y_d, e_b, cd, st_in = [], [], [], []
            for e in range(2):
                he = 2 * pair + e
                acs_b = jnp.broadcast_to(aux[:, AUX_ACS + he:AUX_ACS + he + 1], (CHUNK, CHUNK))
                acs_row = auxt[AUX_ACS + he:AUX_ACS + he + 1, :]
                dt_row = auxt[AUX_DT + he:AUX_DT + he + 1, :]
                acs_last = acs_row[:, CHUNK - 1:CHUNK]
                decay = jnp.exp(jnp.where(causal, acs_b - acs_row, -jnp.inf))
                m = (cb * decay * dt_row).astype(BF16)
                y_d.append(_dot(m, xs_bf))
                e_b.append(jnp.exp(acs_b))
                cd.append(jnp.exp(acs_last))
                w_row = dt_row * jnp.exp(acs_last - acs_row)
                st_in.append(_dot((b_gt * w_row).astype(BF16), xs_bf))
            st = st_ref[pair]
            y_off = _dot(c_g, st.astype(BF16)) * jnp.where(lo_half, e_b[0], e_b[1])
            st_ref[pair] = (st * jnp.where(lo_half, cd[0], cd[1])
                            + jnp.where(lo_half, st_in[0], st_in[1]))
            y = jnp.where(lo_half, y_d[0], y_d[1]) + y_off
            ys.append(y + dskip_ref[:, LANES * pair:LANES * (pair + 1)] * xs_bf.astype(F32))
        yg = jnp.concatenate(ys, axis=1)
        zg = z_ref[rows, gw * g:gw * (g + 1)].astype(F32)
        gg = yg * (zg * jax.nn.sigmoid(zg))
        gg = gg * lax.rsqrt(jnp.mean(gg * gg, axis=-1, keepdims=True) + EPS)
        outs.append(gg * nw_ref[:, gw * g:gw * (g + 1)])
    return jnp.concatenate(outs, axis=1)


def _fox_kernel(q_ref, k_ref, v_ref, auxt_ref, nw_ref, o_ref, m_ref, acc_ref):
    qi = pl.program_id(1)
    q0 = pl.multiple_of(qi * Q_TILE, Q_TILE)
    n_sub = Q_TILE // K_TILE

    def scores(he, k0, rows):
        cols = slice(LANES * he, LANES * (he + 1))
        ck = auxt_ref[AUX_CUM + he:AUX_CUM + he + 1, pl.ds(k0, K_TILE)]
        return _dot_nt(q_ref[rows, cols], k_ref[pl.ds(k0, K_TILE), cols]) - ck

    def update(he, s, k0, rows, first):
        bmax = jnp.max(s, axis=-1, keepdims=True)
        if first:
            m_new = jnp.broadcast_to(bmax, (s.shape[0], LANES))
        else:
            m_old = m_ref[he, rows, :]
            m_new = jnp.maximum(m_old, bmax)
        m_ref[he, rows, :] = m_new
        p = jnp.exp2(s - jnp.concatenate([m_new] * (K_TILE // LANES), axis=1)).astype(BF16)
        pv = _dot(p, v_ref[pl.ds(k0, K_TILE), LANES * he:LANES * (he + 1)])
        if first:
            acc_ref[he, rows, :] = pv
        else:
            acc_ref[he, rows, :] = jnp.exp2(m_old - m_new) * acc_ref[he, rows, :] + pv

    for d in range(n_sub):
        rows = slice(K_TILE * d, Q_TILE)
        nrows = Q_TILE - K_TILE * d
        causal = (lax.broadcasted_iota(jnp.int32, (nrows, K_TILE), 0)
                  >= lax.broadcasted_iota(jnp.int32, (nrows, K_TILE), 1))
        for he in range(N_HEADS):
            k0 = pl.multiple_of(q0 + K_TILE * d, K_TILE)
            s = jnp.where(causal, scores(he, k0, rows), -jnp.inf)
            update(he, s, k0, rows, first=(d == 0))

    def body(j, carry):
        k0 = pl.multiple_of(j * K_TILE, K_TILE)
        for he in range(N_HEADS):
            update(he, scores(he, k0, slice(None)), k0, slice(None), first=False)
        return carry

    lax.fori_loop(0, qi * n_sub, body, 0)
    lane = lax.broadcasted_iota(jnp.int32, (Q_TILE, LANES), 1)
    lo_half = lane < HEAD_DIM
    pair_outs = []
    for pair in range(N_HEADS // 2):
        o0, o1 = (acc_ref[2 * pair + e] / acc_ref[2 * pair + e][:, HEAD_DIM:HEAD_DIM + 1] for e in range(2))
        pair_outs.append(jnp.where(lo_half, o0, pltpu.roll(o1, HEAD_DIM, axis=1)))
    y = jnp.concatenate(pair_outs, axis=1)
    o_ref[...] = _rms(y, nw_ref[...]).astype(BF16)


def _fox(q, k, v, auxt, nw):
    b, s, dp = q.shape
    return pl.pallas_call(
        _fox_kernel,
        grid=(b, s // Q_TILE),
        in_specs=[
            pl.BlockSpec((None, Q_TILE, dp), lambda bi, qi: (bi, qi, 0)),
            pl.BlockSpec((None, s, dp), lambda bi, qi: (bi, 0, 0)),
            pl.BlockSpec((None, s, dp), lambda bi, qi: (bi, 0, 0)),
            pl.BlockSpec((None, AUX_ROWS, s), lambda bi, qi: (bi, 0, 0)),
            _const_spec((1, D_FOX)),
        ],
        out_specs=pl.BlockSpec((None, Q_TILE, D_FOX), lambda bi, qi: (bi, qi, 0)),
        out_shape=jax.ShapeDtypeStruct((b, s, D_FOX), BF16),
        scratch_shapes=[pltpu.VMEM((N_HEADS, Q_TILE, LANES), F32),
                        pltpu.VMEM((N_HEADS, Q_TILE, LANES), F32)],
        compiler_params=pltpu.CompilerParams(
            dimension_semantics=("arbitrary", "arbitrary"), vmem_limit_bytes=VMEM_LIMIT),
        name="fox",
    )(q, k, v, auxt, nw)


def _ssd_tail_kernel(tiles_per_seq,
                     xbc_ref, z_ref, aux_ref, auxt_ref, dskip_ref, snw_ref,
                     h_ref, yf_ref, p_ref, wos_ref, wof_ref, n2_ref, wg_ref, wu_ref, wd_ref,
                     pn_ref, pg_ref, pu_ref, fn_ref, o_ref, acc_ref, st_ref, ybuf_ref):
    i = pl.program_id(0)
    slot = i % 2

    @pl.when(i == 0)
    def _():
        ybuf_ref[...] = jnp.zeros_like(ybuf_ref)

    @pl.when(i % tiles_per_seq == 0)
    def _():
        st_ref[...] = jnp.zeros_like(st_ref)

    ys = ybuf_ref[1 - slot]
    h2 = h_ref[...] + _dot(ys, wos_ref[...]) + _dot(yf_ref[...], wof_ref[...])
    xn = _rms(h2, n2_ref[...]).astype(BF16)
    for ci in range(TOKEN_TILE // CHUNK):
        y = _ssd_chunk(xbc_ref, z_ref, aux_ref, auxt_ref, dskip_ref, snw_ref, st_ref, CHUNK * ci)
        ybuf_ref[slot, CHUNK * ci:CHUNK * (ci + 1), :] = y.astype(BF16)
    _swiglu_into(xn, wg_ref, wu_ref, wd_ref, acc_ref)
    h3 = h2 + 0.5 * acc_ref[...]
    gate = jax.nn.sigmoid(_dot(_rms(h3, pn_ref[...]).astype(BF16), pg_ref[...]))
    h4 = h3 + gate * _dot(p_ref[...].astype(BF16), pu_ref[...])
    o_ref[...] = _rms(h4, fn_ref[...])


def _ssd_tail(xbc, z, aux, auxt, dskip_row, snw, h1, yf, p3d, wos, wof, n2, wg, wu, wd, pn, pg, pu, fn):
    b, s, _ = h1.shape
    tps = s // TOKEN_TILE
    n_tiles = b * tps
    cur = lambda d: pl.BlockSpec(
        (None, TOKEN_TILE, d),
        lambda i: (jnp.minimum(i, n_tiles - 1) // tps, jnp.minimum(i, n_tiles - 1) % tps, 0))
    prev = lambda d: pl.BlockSpec(
        (None, TOKEN_TILE, d),
        lambda i: (jnp.maximum(i - 1, 0) // tps, jnp.maximum(i - 1, 0) % tps, 0))
    return pl.pallas_call(
        functools.partial(_ssd_tail_kernel, tps),
        grid=(n_tiles + 1,),
        in_specs=[
            cur(D_CONV), cur(D_SSM), cur(LANES),
            pl.BlockSpec((None, AUX_ROWS, TOKEN_TILE),
                         lambda i: (jnp.minimum(i, n_tiles - 1) // tps, 0, jnp.minimum(i, n_tiles - 1) % tps)),
            _const_spec((1, D_SSM)), _const_spec((1, D_SSM)),
            prev(D_MODEL), prev(D_FOX), prev(D_PLE),
            _const_spec((D_SSM, D_MODEL)), _const_spec((D_FOX, D_MODEL)),
            _const_spec((1, D_MODEL)),
            _const_spec((D_MODEL, D_FF)), _const_spec((D_MODEL, D_FF)), _const_spec((D_FF, D_MODEL)),
            _const_spec((1, D_MODEL)), _const_spec((D_MODEL, D_MODEL)),
            _const_spec((D_PLE, D_MODEL)), _const_spec((1, D_MODEL)),
        ],
        out_specs=prev(D_MODEL),
        out_shape=jax.ShapeDtypeStruct((b, s, D_MODEL), F32),
        scratch_shapes=[pltpu.VMEM((TOKEN_TILE, D_MODEL), F32),
                        pltpu.VMEM((N_HEADS // 2, D_STATE, LANES), F32),
                        pltpu.VMEM((2, TOKEN_TILE, D_SSM), BF16)],
        compiler_params=pltpu.CompilerParams(
            dimension_semantics=("arbitrary",), vmem_limit_bytes=VMEM_LIMIT),
        name="ssd_tail",
    )(xbc, z, aux, auxt, dskip_row, snw, h1, yf, p3d, wos, wof, n2, wg, wu, wd, pn, pg, pu, fn)


def _pad_lanes(v):
    return jnp.pad(v, (0, LANES - v.shape[0])).reshape(1, LANES).astype(F32)


def kernel(x, p, ffn1_norm, ffn1_w_gate, ffn1_w_up, ffn1_w_down, mix_norm, w_in, conv_w, conv_b, dt_bias, a_log, d_skip, ssd_norm, fox_f_bias, fox_norm, w_out, ffn2_norm, ffn2_w_gate, ffn2_w_up, ffn2_w_down, ple_norm, ple_w_gate, ple_w_up, final_norm):
    bsz, seq, _ = x.shape
    depth = p.shape[0]
    row = lambda v: v.reshape(1, -1).astype(F32)
    h = x.reshape(bsz * seq, D_MODEL)
    for i in range(depth):
        h1 = _ffn1(h, row(ffn1_norm[i]), ffn1_w_gate[i].astype(BF16), ffn1_w_up[i].astype(BF16),
                   ffn1_w_down[i].astype(BF16))

        w = w_in[i]
        o_xbc, o_dt = D_SSM + D_CONV, D_SSM + D_CONV + N_HEADS
        o_q, o_v = o_dt + D_FOX, o_dt + 3 * D_FOX
        w_dt, w_f = w[:, o_xbc:o_dt], w[:, o_v:]
        gates = jnp.concatenate(
            [w_dt, w_dt, w_f, jnp.zeros((D_MODEL, LANES - 3 * N_HEADS), w.dtype)], axis=1)
        win = jnp.concatenate(
            [w[:, :o_xbc], w[:, o_dt:o_q] * (HEAD_DIM ** -0.5 * LOG2_E), w[:, o_q:o_v], gates],
            axis=1).astype(BF16)
        bias_row = _pad_lanes(jnp.concatenate([dt_bias[i], dt_bias[i], fox_f_bias[i]]))
        a_neg = -jnp.exp(a_log[i].astype(F32))
        a_row = _pad_lanes(jnp.concatenate([jnp.zeros_like(a_neg), a_neg]))
        z, xbc, q, k, v, aux, auxt = _inproj(
            h1.reshape(bsz, seq, D_MODEL), row(mix_norm[i]), win,
            conv_w[i].astype(F32), row(conv_b[i]), bias_row, a_row)

        y_fox = _fox(q, k, v, auxt, row(fox_norm[i]))

        wo = w_out[i].astype(BF16)
        assert depth == 1
        h = _ssd_tail(xbc, z, aux, auxt, row(jnp.repeat(d_skip[i], HEAD_DIM)), row(ssd_norm[i]),
                      h1.reshape(bsz, seq, D_MODEL), y_fox, p[i], wo[:D_SSM], wo[D_SSM:],
                      row(ffn2_norm[i]), ffn2_w_gate[i].astype(BF16), ffn2_w_up[i].astype(BF16),
                      ffn2_w_down[i].astype(BF16), row(ple_norm[i]), ple_w_gate[i].astype(BF16),
                      ple_w_up[i].astype(BF16), row(final_norm))
    return h
```

```python
import functools

import jax
import jax.numpy as jnp
from jax import lax
from jax.experimental import pallas as pl
from jax.experimental.pallas import tpu as pltpu

F32 = jnp.float32
BF16 = jnp.bfloat16

D_MODEL = 1024
D_SSM = 512
HEAD_DIM = 64
N_HEADS = 8
SSM_GROUPS = 2
D_STATE = 128
CONV_K = 4
CHUNK = 128
D_CONV = D_SSM + 2 * SSM_GROUPS * D_STATE
D_FOX = 512
D_HEADS_PAD = 1024
D_FF = 2816
D_PLE = 256
EPS = 1e-6
LOG2_E = 1.4426950408889634

LANES = 128
FF_CHUNK = 256
N_FF_CHUNKS = D_FF // FF_CHUNK
TOKEN_TILE = 512
SEQ_TILE = 512
Q_TILE = 1024
K_TILE = 256
D_PROJ = D_SSM + D_CONV + 3 * D_FOX + LANES
VMEM_LIMIT = 56 * 1024 * 1024

AUX_DT = 0
AUX_ACS = 8
AUX_CUM = 16
AUX_ROWS = 32


def _rms(x, w):
    ms = jnp.mean(x * x, axis=-1, keepdims=True)
    return x * lax.rsqrt(ms + EPS) * w


def _dot(a, b):
    return jnp.dot(a, b, preferred_element_type=F32)


def _dot_nt(a, b):
    return lax.dot_general(a, b, (((1,), (1,)), ((), ())), preferred_element_type=F32)


def _swiglu(xn, wg_ref, wu_ref, wd_ref, a_ref):
    for j in range(N_FF_CHUNKS):
        g = _dot(xn, wg_ref[:, FF_CHUNK * j:FF_CHUNK * (j + 1)])
        u = _dot(xn, wu_ref[:, FF_CHUNK * j:FF_CHUNK * (j + 1)])
        a_ref[:, FF_CHUNK * j:FF_CHUNK * (j + 1)] = (g * jax.nn.sigmoid(g) * u).astype(BF16)
    return _dot(a_ref[...], wd_ref[...])


def _ffn1_kernel(x_ref, nw_ref, wg_ref, wu_ref, wd_ref, o_ref, a_ref):
    x = x_ref[...]
    xn = _rms(x, nw_ref[...]).astype(BF16)
    o_ref[...] = x + 0.5 * _swiglu(xn, wg_ref, wu_ref, wd_ref, a_ref)


def _const_spec(shape):
    nd = len(shape)
    return pl.BlockSpec(shape, lambda *_: (0,) * nd, pipeline_mode=pl.Buffered(1))


def _ffn1(x2d, nw, wg, wu, wd):
    t = x2d.shape[0]
    return pl.pallas_call(
        _ffn1_kernel,
        grid=(t // TOKEN_TILE,),
        in_specs=[
            pl.BlockSpec((TOKEN_TILE, D_MODEL), lambda i: (i, 0)),
            _const_spec((1, D_MODEL)),
            _const_spec((D_MODEL, D_FF)),
            _const_spec((D_MODEL, D_FF)),
            _const_spec((D_FF, D_MODEL)),
        ],
        out_specs=pl.BlockSpec((TOKEN_TILE, D_MODEL), lambda i: (i, 0)),
        out_shape=jax.ShapeDtypeStruct((t, D_MODEL), F32),
        scratch_shapes=[pltpu.VMEM((TOKEN_TILE, D_FF), BF16)],
        compiler_params=pltpu.CompilerParams(
            dimension_semantics=("arbitrary",), vmem_limit_bytes=VMEM_LIMIT),
        name="ffn1",
    )(x2d, nw, wg, wu, wd)


def _split3(x):
    hi = x.astype(BF16)
    r1 = x - hi.astype(F32)
    mid = r1.astype(BF16)
    lo = (r1 - mid.astype(F32)).astype(BF16)
    return hi, mid, lo


def _inproj_kernel(h_ref, nw_ref, win_ref, cw_ref, cb_ref, bias_ref, arow_ref,
                   z_ref, xbc_ref, q_ref, k_ref, v_ref, aux_ref, auxt_ref,
                   xpad_ref, run_ref):
    s_idx = pl.program_id(1)

    @pl.when(s_idx == 0)
    def _():
        xpad_ref[0:8, :] = jnp.zeros((8, D_CONV), F32)
        run_ref[...] = jnp.zeros_like(run_ref)

    u = _rms(h_ref[...], nw_ref[...]).astype(BF16)
    c0 = 0
    z_ref[...] = _dot(u, win_ref[:, c0:c0 + D_SSM]).astype(BF16)
    c0 += D_SSM
    raw = _dot(u, win_ref[:, c0:c0 + D_CONV])
    c0 += D_CONV
    lane = lax.broadcasted_iota(jnp.int32, (SEQ_TILE, LANES), 1)
    lo_half = lane < HEAD_DIM
    for dst_ref, fill in ((q_ref, 0.0), (k_ref, 0.0), (v_ref, 1.0)):
        full = _dot(u, win_ref[:, c0:c0 + D_FOX])
        c0 += D_FOX
        for pair in range(N_HEADS // 2):
            blk = full[:, LANES * pair:LANES * (pair + 1)]
            swapped = pltpu.roll(blk, HEAD_DIM, axis=1)
            dst_ref[:, LANES * 2 * pair:LANES * (2 * pair + 1)] = jnp.where(lo_half, blk, fill).astype(BF16)
            dst_ref[:, LANES * (2 * pair + 1):LANES * (2 * pair + 2)] = jnp.where(lo_half, swapped, fill).astype(BF16)
    small = _dot(u, win_ref[:, c0:c0 + LANES])

    xpad_ref[8:, :] = raw
    y = raw * cw_ref[CONV_K - 1:CONV_K, :] + cb_ref[...]
    for sh in range(1, CONV_K):
        y = y + xpad_ref[8 - sh:8 - sh + SEQ_TILE, :] * cw_ref[CONV_K - 1 - sh:CONV_K - sh, :]
    xpad_ref[0:8, :] = raw[SEQ_TILE - 8:, :]
    xbc_ref[...] = (y * jax.nn.sigmoid(y)).astype(BF16)

    sb = small + bias_ref[...]
    l1p = jnp.log1p(jnp.exp(-jnp.abs(sb)))
    softplus = jnp.maximum(sb, 0.0) + l1p
    log_sig = -(jnp.maximum(-sb, 0.0) + l1p)
    dmat = jnp.where(lane < AUX_CUM, softplus * arow_ref[...],
                     jnp.where(lane < AUX_CUM + N_HEADS, log_sig * LOG2_E, 0.0))
    r = lax.broadcasted_iota(jnp.int32, (CHUNK, CHUNK), 0)
    c = lax.broadcasted_iota(jnp.int32, (CHUNK, CHUNK), 1)
    tri = (r >= c).astype(BF16)
    lane_c = lax.broadcasted_iota(jnp.int32, (CHUNK, LANES), 1)
    lane_r = lax.broadcasted_iota(jnp.int32, (1, LANES), 1)
    run = run_ref[0:1, :]
    pieces = []
    for ci in range(SEQ_TILE // CHUNK):
        rows = slice(CHUNK * ci, CHUNK * (ci + 1))
        hi, mid, lo = _split3(dmat[rows, :])
        cs = _dot(tri, hi) + _dot(tri, mid) + _dot(tri, lo)
        pieces.append(jnp.where(lane_c < AUX_ACS, softplus[rows, :], cs + run))
        run = run + jnp.where(lane_r >= AUX_CUM, cs[CHUNK - 1:CHUNK, :], 0.0)
    run_ref[...] = jnp.broadcast_to(run, run_ref.shape)
    aux = jnp.concatenate(pieces, axis=0)
    aux_ref[...] = aux
    auxt_ref[...] = aux.T[0:AUX_ROWS, :]


def _inproj(h1, nw, win, cw, cb, bias_row, a_row):
    b, s, _ = h1.shape
    tile = lambda d: pl.BlockSpec((None, SEQ_TILE, d), lambda bi, si: (bi, si, 0))
    outs = [
        jax.ShapeDtypeStruct((b, s, D_SSM), BF16),
        jax.ShapeDtypeStruct((b, s, D_CONV), BF16),
        jax.ShapeDtypeStruct((b, s, D_HEADS_PAD), BF16),
        jax.ShapeDtypeStruct((b, s, D_HEADS_PAD), BF16),
        jax.ShapeDtypeStruct((b, s, D_HEADS_PAD), BF16),
        jax.ShapeDtypeStruct((b, s, LANES), F32),
        jax.ShapeDtypeStruct((b, AUX_ROWS, s), F32),
    ]
    return pl.pallas_call(
        _inproj_kernel,
        grid=(b, s // SEQ_TILE),
        in_specs=[
            tile(D_MODEL),
            _const_spec((1, D_MODEL)),
            _const_spec((D_MODEL, D_PROJ)),
            _const_spec((CONV_K, D_CONV)),
            _const_spec((1, D_CONV)),
            _const_spec((1, LANES)),
            _const_spec((1, LANES)),
        ],
        out_specs=[tile(D_SSM), tile(D_CONV), tile(D_HEADS_PAD), tile(D_HEADS_PAD), tile(D_HEADS_PAD),
                   tile(LANES),
                   pl.BlockSpec((None, AUX_ROWS, SEQ_TILE), lambda bi, si: (bi, 0, si))],
        out_shape=outs,
        scratch_shapes=[pltpu.VMEM((SEQ_TILE + 8, D_CONV), F32), pltpu.VMEM((8, LANES), F32)],
        compiler_params=pltpu.CompilerParams(
            dimension_semantics=("arbitrary", "arbitrary"), vmem_limit_bytes=VMEM_LIMIT),
        name="inproj",
    )(h1, nw, win, cw, cb, bias_row, a_row)


def _ssd_chunk(xbc_ref, z_ref, aux_ref, auxt_ref, dskip_ref, nw_ref, st_ref, l0):
    rows = slice(l0, l0 + CHUNK)
    r = lax.broadcasted_iota(jnp.int32, (CHUNK, CHUNK), 0)
    c = lax.broadcasted_iota(jnp.int32, (CHUNK, CHUNK), 1)
    causal = r >= c
    lo_half = c < HEAD_DIM
    hpg = N_HEADS // SSM_GROUPS
    gw = D_SSM // SSM_GROUPS
    aux = aux_ref[rows, :]
    auxt = auxt_ref[:, rows]
    outs = []
    for g in range(SSM_GROUPS):
        b_g = xbc_ref[rows, D_SSM + D_STATE * g:D_SSM + D_STATE * (g + 1)]
        c_g = xbc_ref[rows, D_SSM + D_STATE * (SSM_GROUPS + g):D_SSM + D_STATE * (SSM_GROUPS + g + 1)]
        cb = _dot_nt(c_g, b_g)
        b_gt = b_g.astype(F32).T
        ys = []
        for pp in range(hpg // 2):
            pair = g * (hpg // 2) + pp
            xs_bf = xbc_ref[rows, LANES * pair:LANES * (pair + 1)]
            y_d, e_b, cd, st_in = [], [], [], []
            for e in range(2):
                he = 2 * pair + e
                acs_b = jnp.broadcast_to(aux[:, AUX_ACS + he:AUX_ACS + he + 1], (CHUNK, CHUNK))
                acs_row = auxt[AUX_ACS + he:AUX_ACS + he + 1, :]
                dt_row = auxt[AUX_DT + he:AUX_DT + he + 1, :]
                acs_last = acs_row[:, CHUNK - 1:CHUNK]
                decay = jnp.exp(jnp.where(causal, acs_b - acs_row, -jnp.inf))
                m = (cb * decay * dt_row).astype(BF16)
                y_d.append(_dot(m, xs_bf))
                e_b.append(jnp.exp(acs_b))
                cd.append(jnp.exp(acs_last))
                w_row = dt_row * jnp.exp(acs_last - acs_row)
                st_in.append(_dot((b_gt * w_row).astype(BF16), xs_bf))
            st = st_ref[pair]
            y_off = _dot(c_g, st.astype(BF16)) * jnp.where(lo_half, e_b[0], e_b[1])
            st_ref[pair] = (st * jnp.where(lo_half, cd[0], cd[1])
                            + jnp.where(lo_half, st_in[0], st_in[1]))
            y = jnp.where(lo_half, y_d[0], y_d[1]) + y_off
            ys.append(y + dskip_ref[:, LANES * pair:LANES * (pair + 1)] * xs_bf.astype(F32))
        yg = jnp.concatenate(ys, axis=1)
        zg = z_ref[rows, gw * g:gw * (g + 1)].astype(F32)
        gg = yg * (zg * jax.nn.sigmoid(zg))
        gg = gg * lax.rsqrt(jnp.mean(gg * gg, axis=-1, keepdims=True) + EPS)
        outs.append(gg * nw_ref[:, gw * g:gw * (g + 1)])
    return jnp.concatenate(outs, axis=1)


def _fox_kernel(q_ref, k_ref, v_ref, auxt_ref, nw_ref, o_ref, m_ref, acc_ref):
    qi = pl.program_id(1)
    q0 = pl.multiple_of(qi * Q_TILE, Q_TILE)
    n_sub = Q_TILE // K_TILE

    def scores(he, k0, rows):
        cols = slice(LANES * he, LANES * (he + 1))
        ck = auxt_ref[AUX_CUM + he:AUX_CUM + he + 1, pl.ds(k0, K_TILE)]
        return _dot_nt(q_ref[rows, cols], k_ref[pl.ds(k0, K_TILE), cols]) - ck

    def update(he, s, k0, rows, first):
        bmax = jnp.max(s, axis=-1, keepdims=True)
        if first:
            m_new = jnp.broadcast_to(bmax, (s.shape[0], LANES))
        else:
            m_old = m_ref[he, rows, :]
            m_new = jnp.maximum(m_old, bmax)
        m_ref[he, rows, :] = m_new
        p = jnp.exp2(s - jnp.concatenate([m_new] * (K_TILE // LANES), axis=1)).astype(BF16)
        pv = _dot(p, v_ref[pl.ds(k0, K_TILE), LANES * he:LANES * (he + 1)])
        if first:
            acc_ref[he, rows, :] = pv
        else:
            acc_ref[he, rows, :] = jnp.exp2(m_old - m_new) * acc_ref[he, rows, :] + pv

    for d in range(n_sub):
        rows = slice(K_TILE * d, Q_TILE)
        nrows = Q_TILE - K_TILE * d
        causal = (lax.broadcasted_iota(jnp.int32, (nrows, K_TILE), 0)
                  >= lax.broadcasted_iota(jnp.int32, (nrows, K_TILE), 1))
        for he in range(N_HEADS):
            k0 = pl.multiple_of(q0 + K_TILE * d, K_TILE)
            s = jnp.where(causal, scores(he, k0, rows), -jnp.inf)
            update(he, s, k0, rows, first=(d == 0))

    def body(j, carry):
        k0 = pl.multiple_of(j * K_TILE, K_TILE)
        for he in range(N_HEADS):
            update(he, scores(he, k0, slice(None)), k0, slice(None), first=False)
        return carry

    lax.fori_loop(0, qi * n_sub, body, 0)
    lane = lax.broadcasted_iota(jnp.int32, (Q_TILE, LANES), 1)
    lo_half = lane < HEAD_DIM
    pair_outs = []
    for pair in range(N_HEADS // 2):
        a0, a1 = acc_ref[2 * pair], acc_ref[2 * pair + 1]
        swapped = pltpu.roll(jnp.where(lo_half, a1, a0), HEAD_DIM, axis=1)
        pair_outs.append(jnp.where(lo_half, a0, swapped) / jnp.where(lo_half, swapped, a1))
    y = jnp.concatenate(pair_outs, axis=1)
    o_ref[...] = _rms(y, nw_ref[...]).astype(BF16)


def _fox(q, k, v, auxt, nw):
    b, s, dp = q.shape
    return pl.pallas_call(
        _fox_kernel,
        grid=(b, s // Q_TILE),
        in_specs=[
            pl.BlockSpec((None, Q_TILE, dp), lambda bi, qi: (bi, qi, 0)),
            pl.BlockSpec((None, s, dp), lambda bi, qi: (bi, 0, 0)),
            pl.BlockSpec((None, s, dp), lambda bi, qi: (bi, 0, 0)),
            pl.BlockSpec((None, AUX_ROWS, s), lambda bi, qi: (bi, 0, 0)),
            _const_spec((1, D_FOX)),
        ],
        out_specs=pl.BlockSpec((None, Q_TILE, D_FOX), lambda bi, qi: (bi, qi, 0)),
        out_shape=jax.ShapeDtypeStruct((b, s, D_FOX), BF16),
        scratch_shapes=[pltpu.VMEM((N_HEADS, Q_TILE, LANES), F32),
                        pltpu.VMEM((N_HEADS, Q_TILE, LANES), F32)],
        compiler_params=pltpu.CompilerParams(
            dimension_semantics=("arbitrary", "arbitrary"), vmem_limit_bytes=VMEM_LIMIT),
        name="fox",
    )(q, k, v, auxt, nw)


def _ssd_tail_kernel(tiles_per_seq,
                     xbc_ref, z_ref, aux_ref, auxt_ref, dskip_ref, snw_ref,
                     h_ref, yf_ref, p_ref, wos_ref, wof_ref, n2_ref, wg_ref, wu_ref, wd_ref,
                     pn_ref, pg_ref, pu_ref, fn_ref, o_ref, a_ref, st_ref, ybuf_ref):
    i = pl.program_id(0)
    slot = i % 2

    @pl.when(i == 0)
    def _():
        ybuf_ref[...] = jnp.zeros_like(ybuf_ref)

    @pl.when(i % tiles_per_seq == 0)
    def _():
        st_ref[...] = jnp.zeros_like(st_ref)

    ys = ybuf_ref[1 - slot]
    h2 = h_ref[...] + _dot(ys, wos_ref[...]) + _dot(yf_ref[...], wof_ref[...])
    xn = _rms(h2, n2_ref[...]).astype(BF16)
    for ci in range(TOKEN_TILE // CHUNK):
        y = _ssd_chunk(xbc_ref, z_ref, aux_ref, auxt_ref, dskip_ref, snw_ref, st_ref, CHUNK * ci)
        ybuf_ref[slot, CHUNK * ci:CHUNK * (ci + 1), :] = y.astype(BF16)
    h3 = h2 + 0.5 * _swiglu(xn, wg_ref, wu_ref, wd_ref, a_ref)
    gate = jax.nn.sigmoid(_dot(_rms(h3, pn_ref[...]).astype(BF16), pg_ref[...]))
    h4 = h3 + gate * _dot(p_ref[...].astype(BF16), pu_ref[...])
    o_ref[...] = _rms(h4, fn_ref[...])


def _ssd_tail(xbc, z, aux, auxt, dskip_row, snw, h1, yf, p3d, wos, wof, n2, wg, wu, wd, pn, pg, pu, fn):
    b, s, _ = h1.shape
    tps = s // TOKEN_TILE
    n_tiles = b * tps
    cur = lambda d: pl.BlockSpec(
        (None, TOKEN_TILE, d),
        lambda i: (jnp.minimum(i, n_tiles - 1) // tps, jnp.minimum(i, n_tiles - 1) % tps, 0))
    prev = lambda d: pl.BlockSpec(
        (None, TOKEN_TILE, d),
        lambda i: (jnp.maximum(i - 1, 0) // tps, jnp.maximum(i - 1, 0) % tps, 0))
    return pl.pallas_call(
        functools.partial(_ssd_tail_kernel, tps),
        grid=(n_tiles + 1,),
        in_specs=[
            cur(D_CONV), cur(D_SSM), cur(LANES),
            pl.BlockSpec((None, AUX_ROWS, TOKEN_TILE),
                         lambda i: (jnp.minimum(i, n_tiles - 1) // tps, 0, jnp.minimum(i, n_tiles - 1) % tps)),
            _const_spec((1, D_SSM)), _const_spec((1, D_SSM)),
            prev(D_MODEL), prev(D_FOX), prev(D_PLE),
            _const_spec((D_SSM, D_MODEL)), _const_spec((D_FOX, D_MODEL)),
            _const_spec((1, D_MODEL)),
            _const_spec((D_MODEL, D_FF)), _const_spec((D_MODEL, D_FF)), _const_spec((D_FF, D_MODEL)),
            _const_spec((1, D_MODEL)), _const_spec((D_MODEL, D_MODEL)),
            _const_spec((D_PLE, D_MODEL)), _const_spec((1, D_MODEL)),
        ],
        out_specs=prev(D_MODEL),
        out_shape=jax.ShapeDtypeStruct((b, s, D_MODEL), F32),
        scratch_shapes=[pltpu.VMEM((TOKEN_TILE, D_FF), BF16),
                        pltpu.VMEM((N_HEADS // 2, D_STATE, LANES), F32),
                        pltpu.VMEM((2, TOKEN_TILE, D_SSM), BF16)],
        compiler_params=pltpu.CompilerParams(
            dimension_semantics=("arbitrary",), vmem_limit_bytes=VMEM_LIMIT),
        name="ssd_tail",
    )(xbc, z, aux, auxt, dskip_row, snw, h1, yf, p3d, wos, wof, n2, wg, wu, wd, pn, pg, pu, fn)


def _pad_lanes(v):
    return jnp.pad(v, (0, LANES - v.shape[0])).reshape(1, LANES).astype(F32)


def kernel(x, p, ffn1_norm, ffn1_w_gate, ffn1_w_up, ffn1_w_down, mix_norm, w_in, conv_w, conv_b, dt_bias, a_log, d_skip, ssd_norm, fox_f_bias, fox_norm, w_out, ffn2_norm, ffn2_w_gate, ffn2_w_up, ffn2_w_down, ple_norm, ple_w_gate, ple_w_up, final_norm):
    bsz, seq, _ = x.shape
    depth = p.shape[0]
    row = lambda v: v.reshape(1, -1).astype(F32)
    h = x.reshape(bsz * seq, D_MODEL)
    for i in range(depth):
        h1 = _ffn1(h, row(ffn1_norm[i]), ffn1_w_gate[i].astype(BF16), ffn1_w_up[i].astype(BF16),
                   ffn1_w_down[i].astype(BF16))

        w = w_in[i]
        o_xbc, o_dt = D_SSM + D_CONV, D_SSM + D_CONV + N_HEADS
        o_q, o_v = o_dt + D_FOX, o_dt + 3 * D_FOX
        w_dt, w_f = w[:, o_xbc:o_dt], w[:, o_v:]
        gates = jnp.concatenate(
            [w_dt, w_dt, w_f, jnp.zeros((D_MODEL, LANES - 3 * N_HEADS), w.dtype)], axis=1)
        win = jnp.concatenate(
            [w[:, :o_xbc], w[:, o_dt:o_q] * (HEAD_DIM ** -0.5 * LOG2_E), w[:, o_q:o_v], gates],
            axis=1).astype(BF16)
        bias_row = _pad_lanes(jnp.concatenate([dt_bias[i], dt_bias[i], fox_f_bias[i]]))
        a_neg = -jnp.exp(a_log[i].astype(F32))
        a_row = _pad_lanes(jnp.concatenate([jnp.zeros_like(a_neg), a_neg]))
        z, xbc, q, k, v, aux, auxt = _inproj(
            h1.reshape(bsz, seq, D_MODEL), row(mix_norm[i]), win,
            conv_w[i].astype(F32), row(conv_b[i]), bias_row, a_row)

        y_fox = _fox(q, k, v, auxt, row(fox_norm[i]))

        wo = w_out[i].astype(BF16)
        assert depth == 1
        h = _ssd_tail(xbc, z, aux, auxt, row(jnp.repeat(d_skip[i], HEAD_DIM)), row(ssd_norm[i]),
                      h1.reshape(bsz, seq, D_MODEL), y_fox, p[i], wo[:D_SSM], wo[D_SSM:],
                      row(ffn2_norm[i]), ffn2_w_gate[i].astype(BF16), ffn2_w_up[i].astype(BF16),
                      ffn2_w_down[i].astype(BF16), row(ple_norm[i]), ple_w_gate[i].astype(BF16),
                      ple_w_up[i].astype(BF16), row(final_norm))
    return h
```

```python
import functools

import jax
import jax.numpy as jnp
from jax import lax
from jax.experimental import pallas as pl
from jax.experimental.pallas import tpu as pltpu

F32 = jnp.float32
BF16 = jnp.bfloat16

D_MODEL = 1024
D_SSM = 512
HEAD_DIM = 64
N_HEADS = 8
SSM_GROUPS = 2
D_STATE = 128
CONV_K = 4
CHUNK = 128
D_CONV = D_SSM + 2 * SSM_GROUPS * D_STATE
D_FOX = 512
D_HEADS_PAD = 1024
D_FF = 2816
D_PLE = 256
EPS = 1e-6
LOG2_E = 1.4426950408889634

LANES = 128
FF_CHUNK = 256
N_FF_CHUNKS = D_FF // FF_CHUNK
FFN1_TILE = 1024
TOKEN_TILE = 512
SEQ_TILE = 512
Q_TILE = 1024
K_TILE = 256
D_PROJ = D_SSM + D_CONV + 3 * D_FOX + LANES
VMEM_LIMIT = 56 * 1024 * 1024

AUX_DT = 0
AUX_ACS = 8
AUX_CUM = 16
AUX_ROWS = 32


def _rms(x, w):
    ms = jnp.mean(x * x, axis=-1, keepdims=True)
    return x * lax.rsqrt(ms + EPS) * w


def _dot(a, b):
    return jnp.dot(a, b, preferred_element_type=F32)


def _dot_nt(a, b):
    return lax.dot_general(a, b, (((1,), (1,)), ((), ())), preferred_element_type=F32)


def _swiglu(xn, wg_ref, wu_ref, wd_ref, a_ref, between=None):
    for j in range(N_FF_CHUNKS):
        g = _dot(xn, wg_ref[:, FF_CHUNK * j:FF_CHUNK * (j + 1)])
        u = _dot(xn, wu_ref[:, FF_CHUNK * j:FF_CHUNK * (j + 1)])
        if between is not None:
            between(j)
        a_ref[:, FF_CHUNK * j:FF_CHUNK * (j + 1)] = (g * jax.nn.sigmoid(g) * u).astype(BF16)
    return _dot(a_ref[...], wd_ref[...])


def _ffn1_kernel(x_ref, nw_ref, wg_ref, wu_ref, wd_ref, o_ref, a_ref):
    x = x_ref[...]
    xn = _rms(x, nw_ref[...]).astype(BF16)
    o_ref[...] = x + 0.5 * _swiglu(xn, wg_ref, wu_ref, wd_ref, a_ref)


def _const_spec(shape):
    nd = len(shape)
    return pl.BlockSpec(shape, lambda *_: (0,) * nd, pipeline_mode=pl.Buffered(1))


def _ffn1(x2d, nw, wg, wu, wd):
    t = x2d.shape[0]
    return pl.pallas_call(
        _ffn1_kernel,
        grid=(t // FFN1_TILE,),
        in_specs=[
            pl.BlockSpec((FFN1_TILE, D_MODEL), lambda i: (i, 0)),
            _const_spec((1, D_MODEL)),
            _const_spec((D_MODEL, D_FF)),
            _const_spec((D_MODEL, D_FF)),
            _const_spec((D_FF, D_MODEL)),
        ],
        out_specs=pl.BlockSpec((FFN1_TILE, D_MODEL), lambda i: (i, 0)),
        out_shape=jax.ShapeDtypeStruct((t, D_MODEL), F32),
        scratch_shapes=[pltpu.VMEM((FFN1_TILE, D_FF), BF16)],
        compiler_params=pltpu.CompilerParams(
            dimension_semantics=("arbitrary",), vmem_limit_bytes=VMEM_LIMIT),
        name="ffn1",
    )(x2d, nw, wg, wu, wd)


def _split3(x):
    hi = x.astype(BF16)
    r1 = x - hi.astype(F32)
    mid = r1.astype(BF16)
    lo = (r1 - mid.astype(F32)).astype(BF16)
    return hi, mid, lo


def _inproj_kernel(h_ref, nw_ref, win_ref, cw_ref, cb_ref, bias_ref, arow_ref,
                   z_ref, xbc_ref, q_ref, k_ref, v_ref, aux_ref, auxt_ref,
                   xpad_ref, run_ref):
    s_idx = pl.program_id(1)

    @pl.when(s_idx == 0)
    def _():
        xpad_ref[0:8, :] = jnp.zeros((8, D_CONV), F32)
        run_ref[...] = jnp.zeros_like(run_ref)

    u = _rms(h_ref[...], nw_ref[...]).astype(BF16)
    c0 = 0
    z_ref[...] = _dot(u, win_ref[:, c0:c0 + D_SSM]).astype(BF16)
    c0 += D_SSM
    raw = _dot(u, win_ref[:, c0:c0 + D_CONV])
    c0 += D_CONV
    lane = lax.broadcasted_iota(jnp.int32, (SEQ_TILE, LANES), 1)
    lo_half = lane < HEAD_DIM
    for dst_ref, fill in ((q_ref, 0.0), (k_ref, 0.0), (v_ref, 1.0)):
        full = _dot(u, win_ref[:, c0:c0 + D_FOX])
        c0 += D_FOX
        for pair in range(N_HEADS // 2):
            blk = full[:, LANES * pair:LANES * (pair + 1)]
            swapped = pltpu.roll(blk, HEAD_DIM, axis=1)
            dst_ref[:, LANES * 2 * pair:LANES * (2 * pair + 1)] = jnp.where(lo_half, blk, fill).astype(BF16)
            dst_ref[:, LANES * (2 * pair + 1):LANES * (2 * pair + 2)] = jnp.where(lo_half, swapped, fill).astype(BF16)
    small = _dot(u, win_ref[:, c0:c0 + LANES])

    xpad_ref[8:, :] = raw
    y = raw * cw_ref[CONV_K - 1:CONV_K, :] + cb_ref[...]
    for sh in range(1, CONV_K):
        y = y + xpad_ref[8 - sh:8 - sh + SEQ_TILE, :] * cw_ref[CONV_K - 1 - sh:CONV_K - sh, :]
    xpad_ref[0:8, :] = raw[SEQ_TILE - 8:, :]
    xbc_ref[...] = (y * jax.nn.sigmoid(y)).astype(BF16)

    sb = small + bias_ref[...]
    l1p = jnp.log1p(jnp.exp(-jnp.abs(sb)))
    softplus = jnp.maximum(sb, 0.0) + l1p
    log_sig = -(jnp.maximum(-sb, 0.0) + l1p)
    dmat = jnp.where(lane < AUX_CUM, softplus * arow_ref[...],
                     jnp.where(lane < AUX_CUM + N_HEADS, log_sig * LOG2_E, 0.0))
    r = lax.broadcasted_iota(jnp.int32, (CHUNK, CHUNK), 0)
    c = lax.broadcasted_iota(jnp.int32, (CHUNK, CHUNK), 1)
    tri = (r >= c).astype(BF16)
    lane_c = lax.broadcasted_iota(jnp.int32, (CHUNK, LANES), 1)
    lane_r = lax.broadcasted_iota(jnp.int32, (1, LANES), 1)
    run = run_ref[0:1, :]
    pieces = []
    for ci in range(SEQ_TILE // CHUNK):
        rows = slice(CHUNK * ci, CHUNK * (ci + 1))
        hi, mid, lo = _split3(dmat[rows, :])
        cs = _dot(tri, hi) + _dot(tri, mid) + _dot(tri, lo)
        pieces.append(jnp.where(lane_c < AUX_ACS, softplus[rows, :], cs + run))
        run = run + jnp.where(lane_r >= AUX_CUM, cs[CHUNK - 1:CHUNK, :], 0.0)
    run_ref[...] = jnp.broadcast_to(run, run_ref.shape)
    aux = jnp.concatenate(pieces, axis=0)
    aux_ref[...] = aux
    auxt_ref[...] = aux.T[0:AUX_ROWS, :]


def _inproj(h1, nw, win, cw, cb, bias_row, a_row):
    b, s, _ = h1.shape
    tile = lambda d: pl.BlockSpec((None, SEQ_TILE, d), lambda bi, si: (bi, si, 0))
    outs = [
        jax.ShapeDtypeStruct((b, s, D_SSM), BF16),
        jax.ShapeDtypeStruct((b, s, D_CONV), BF16),
        jax.ShapeDtypeStruct((b, s, D_HEADS_PAD), BF16),
        jax.ShapeDtypeStruct((b, s, D_HEADS_PAD), BF16),
        jax.ShapeDtypeStruct((b, s, D_HEADS_PAD), BF16),
        jax.ShapeDtypeStruct((b, s, LANES), F32),
        jax.ShapeDtypeStruct((b, AUX_ROWS, s), F32),
    ]
    return pl.pallas_call(
        _inproj_kernel,
        grid=(b, s // SEQ_TILE),
        in_specs=[
            tile(D_MODEL),
            _const_spec((1, D_MODEL)),
            _const_spec((D_MODEL, D_PROJ)),
            _const_spec((CONV_K, D_CONV)),
            _const_spec((1, D_CONV)),
            _const_spec((1, LANES)),
            _const_spec((1, LANES)),
        ],
        out_specs=[tile(D_SSM), tile(D_CONV), tile(D_HEADS_PAD), tile(D_HEADS_PAD), tile(D_HEADS_PAD),
                   tile(LANES),
                   pl.BlockSpec((None, AUX_ROWS, SEQ_TILE), lambda bi, si: (bi, 0, si))],
        out_shape=outs,
        scratch_shapes=[pltpu.VMEM((SEQ_TILE + 8, D_CONV), F32), pltpu.VMEM((8, LANES), F32)],
        compiler_params=pltpu.CompilerParams(
            dimension_semantics=("arbitrary", "arbitrary"), vmem_limit_bytes=VMEM_LIMIT),
        name="inproj",
    )(h1, nw, win, cw, cb, bias_row, a_row)


def _ssd_stage1(xbc_ref, rows):
    out = []
    for g in range(SSM_GROUPS):
        b_g = xbc_ref[rows, D_SSM + D_STATE * g:D_SSM + D_STATE * (g + 1)]
        c_g = xbc_ref[rows, D_SSM + D_STATE * (SSM_GROUPS + g):D_SSM + D_STATE * (SSM_GROUPS + g + 1)]
        out.append((_dot_nt(c_g, b_g), b_g.astype(F32).T, c_g))
    return out


def _ssd_stage2(xbc_ref, aux_ref, auxt_ref, st_ref, rows, stage1):
    r = lax.broadcasted_iota(jnp.int32, (CHUNK, CHUNK), 0)
    c = lax.broadcasted_iota(jnp.int32, (CHUNK, CHUNK), 1)
    causal = r >= c
    lo_half = c < HEAD_DIM
    aux = aux_ref[rows, :]
    auxt = auxt_ref[:, rows]
    out = []
    for pair in range(N_HEADS // 2):
        cb, b_gt, c_g = stage1[pair // (N_HEADS // SSM_GROUPS // 2)]
        xs_bf = xbc_ref[rows, LANES * pair:LANES * (pair + 1)]
        y_d, e_b, cd, st_in = [], [], [], []
        for e in range(2):
            he = 2 * pair + e
            acs_b = jnp.broadcast_to(aux[:, AUX_ACS + he:AUX_ACS + he + 1], (CHUNK, CHUNK))
            acs_row = auxt[AUX_ACS + he:AUX_ACS + he + 1, :]
            dt_row = auxt[AUX_DT + he:AUX_DT + he + 1, :]
            acs_last = acs_row[:, CHUNK - 1:CHUNK]
            decay = jnp.exp(jnp.where(causal, acs_b - acs_row, -jnp.inf))
            m = (cb * decay * dt_row).astype(BF16)
            y_d.append(_dot(m, xs_bf))
            e_b.append(jnp.exp(acs_b))
            cd.append(jnp.exp(acs_last))
            w_row = dt_row * jnp.exp(acs_last - acs_row)
            st_in.append(_dot((b_gt * w_row).astype(BF16), xs_bf))
        st = st_ref[pair]
        y_off = _dot(c_g, st.astype(BF16)) * jnp.where(lo_half, e_b[0], e_b[1])
        st_new = st * jnp.where(lo_half, cd[0], cd[1]) + jnp.where(lo_half, st_in[0], st_in[1])
        out.append((jnp.where(lo_half, y_d[0], y_d[1]) + y_off, st_new))
    return out


def _ssd_stage3(xbc_ref, z_ref, dskip_ref, nw_ref, st_ref, rows, stage2):
    gw = D_SSM // SSM_GROUPS
    ppg = N_HEADS // SSM_GROUPS // 2
    outs = []
    for g in range(SSM_GROUPS):
        ys = []
        for pair in range(ppg * g, ppg * (g + 1)):
            y, st_new = stage2[pair]
            st_ref[pair] = st_new
            xs = xbc_ref[rows, LANES * pair:LANES * (pair + 1)].astype(F32)
            ys.append(y + dskip_ref[:, LANES * pair:LANES * (pair + 1)] * xs)
        yg = jnp.concatenate(ys, axis=1)
        zg = z_ref[rows, gw * g:gw * (g + 1)].astype(F32)
        gg = yg * (zg * jax.nn.sigmoid(zg))
        gg = gg * lax.rsqrt(jnp.mean(gg * gg, axis=-1, keepdims=True) + EPS)
        outs.append(gg * nw_ref[:, gw * g:gw * (g + 1)])
    return jnp.concatenate(outs, axis=1)


def _fox_kernel(q_ref, k_ref, v_ref, auxt_ref, nw_ref, o_ref, m_ref, acc_ref):
    qi = pl.program_id(1)
    q0 = pl.multiple_of(qi * Q_TILE, Q_TILE)
    n_sub = Q_TILE // K_TILE

    def scores(he, k0, rows):
        cols = slice(LANES * he, LANES * (he + 1))
        ck = auxt_ref[AUX_CUM + he:AUX_CUM + he + 1, pl.ds(k0, K_TILE)]
        return _dot_nt(q_ref[rows, cols], k_ref[pl.ds(k0, K_TILE), cols]) - ck

    def update(he, s, k0, rows, first):
        bmax = jnp.max(s, axis=-1, keepdims=True)
        if first:
            m_new = jnp.broadcast_to(bmax, (s.shape[0], LANES))
        else:
            m_old = m_ref[he, rows, :]
            m_new = jnp.maximum(m_old, bmax)
        m_ref[he, rows, :] = m_new
        p = jnp.exp2(s - jnp.concatenate([m_new] * (K_TILE // LANES), axis=1)).astype(BF16)
        pv = _dot(p, v_ref[pl.ds(k0, K_TILE), LANES * he:LANES * (he + 1)])
        if first:
            acc_ref[he, rows, :] = pv
        else:
            acc_ref[he, rows, :] = jnp.exp2(m_old - m_new) * acc_ref[he, rows, :] + pv

    for d in range(n_sub):
        rows = slice(K_TILE * d, Q_TILE)
        nrows = Q_TILE - K_TILE * d
        causal = (lax.broadcasted_iota(jnp.int32, (nrows, K_TILE), 0)
                  >= lax.broadcasted_iota(jnp.int32, (nrows, K_TILE), 1))
        for he in range(N_HEADS):
            k0 = pl.multiple_of(q0 + K_TILE * d, K_TILE)
            s = jnp.where(causal, scores(he, k0, rows), -jnp.inf)
            update(he, s, k0, rows, first=(d == 0))

    def body(j, carry):
        for sub in range(n_sub):
            k0 = pl.multiple_of(j * Q_TILE + K_TILE * sub, K_TILE)
            for he in range(N_HEADS):
                update(he, scores(he, k0, slice(None)), k0, slice(None), first=False)
        return carry

    lax.fori_loop(0, qi, body, 0)
    lane = lax.broadcasted_iota(jnp.int32, (Q_TILE, LANES), 1)
    lo_half = lane < HEAD_DIM
    pair_outs = []
    for pair in range(N_HEADS // 2):
        a0, a1 = acc_ref[2 * pair], acc_ref[2 * pair + 1]
        swapped = pltpu.roll(jnp.where(lo_half, a1, a0), HEAD_DIM, axis=1)
        pair_outs.append(jnp.where(lo_half, a0, swapped) / jnp.where(lo_half, swapped, a1))
    y = jnp.concatenate(pair_outs, axis=1)
    o_ref[...] = _rms(y, nw_ref[...]).astype(BF16)


def _fox(q, k, v, auxt, nw):
    b, s, dp = q.shape
    return pl.pallas_call(
        _fox_kernel,
        grid=(b, s // Q_TILE),
        in_specs=[
            pl.BlockSpec((None, Q_TILE, dp), lambda bi, qi: (bi, qi, 0)),
            pl.BlockSpec((None, s, dp), lambda bi, qi: (bi, 0, 0)),
            pl.BlockSpec((None, s, dp), lambda bi, qi: (bi, 0, 0)),
            pl.BlockSpec((None, AUX_ROWS, s), lambda bi, qi: (bi, 0, 0)),
            _const_spec((1, D_FOX)),
        ],
        out_specs=pl.BlockSpec((None, Q_TILE, D_FOX), lambda bi, qi: (bi, qi, 0)),
        out_shape=jax.ShapeDtypeStruct((b, s, D_FOX), BF16),
        scratch_shapes=[pltpu.VMEM((N_HEADS, Q_TILE, LANES), F32),
                        pltpu.VMEM((N_HEADS, Q_TILE, LANES), F32)],
        compiler_params=pltpu.CompilerParams(
            dimension_semantics=("arbitrary", "arbitrary"), vmem_limit_bytes=VMEM_LIMIT),
        name="fox",
    )(q, k, v, auxt, nw)


def _ssd_tail_kernel(tiles_per_seq,
                     xbc_ref, z_ref, aux_ref, auxt_ref, dskip_ref, snw_ref,
                     h_ref, yf_ref, p_ref, wos_ref, wof_ref, n2_ref, wg_ref, wu_ref, wd_ref,
                     pn_ref, pg_ref, pu_ref, fn_ref, o_ref, a_ref, st_ref, ybuf_ref):
    i = pl.program_id(0)
    slot = i % 2

    @pl.when(i == 0)
    def _():
        ybuf_ref[...] = jnp.zeros_like(ybuf_ref)

    @pl.when(i % tiles_per_seq == 0)
    def _():
        st_ref[...] = jnp.zeros_like(st_ref)

    ys = ybuf_ref[1 - slot]
    h2 = h_ref[...] + _dot(ys, wos_ref[...]) + _dot(yf_ref[...], wof_ref[...])
    xn = _rms(h2, n2_ref[...]).astype(BF16)
    n_chunks = TOKEN_TILE // CHUNK
    rows = [slice(CHUNK * ci, CHUNK * (ci + 1)) for ci in range(n_chunks)]
    stage1 = {0: _ssd_stage1(xbc_ref, rows[0])}

    def scan_work(j):
        ci, phase = divmod(j, 2)
        if ci < n_chunks and phase == 0:
            stage2 = _ssd_stage2(xbc_ref, aux_ref, auxt_ref, st_ref, rows[ci], stage1.pop(ci))
            y = _ssd_stage3(xbc_ref, z_ref, dskip_ref, snw_ref, st_ref, rows[ci], stage2)
            ybuf_ref[slot, rows[ci], :] = y.astype(BF16)
        elif ci + 1 < n_chunks and phase == 1:
            stage1[ci + 1] = _ssd_stage1(xbc_ref, rows[ci + 1])

    h3 = h2 + 0.5 * _swiglu(xn, wg_ref, wu_ref, wd_ref, a_ref, between=scan_work)
    gate = jax.nn.sigmoid(_dot(_rms(h3, pn_ref[...]).astype(BF16), pg_ref[...]))
    h4 = h3 + gate * _dot(p_ref[...].astype(BF16), pu_ref[...])
    o_ref[...] = _rms(h4, fn_ref[...])


def _ssd_tail(xbc, z, aux, auxt, dskip_row, snw, h1, yf, p3d, wos, wof, n2, wg, wu, wd, pn, pg, pu, fn):
    b, s, _ = h1.shape
    tps = s // TOKEN_TILE
    n_tiles = b * tps
    cur = lambda d: pl.BlockSpec(
        (None, TOKEN_TILE, d),
        lambda i: (jnp.minimum(i, n_tiles - 1) // tps, jnp.minimum(i, n_tiles - 1) % tps, 0))
    prev = lambda d: pl.BlockSpec(
        (None, TOKEN_TILE, d),
        lambda i: (jnp.maximum(i - 1, 0) // tps, jnp.maximum(i - 1, 0) % tps, 0))
    return pl.pallas_call(
        functools.partial(_ssd_tail_kernel, tps),
        grid=(n_tiles + 1,),
        in_specs=[
            cur(D_CONV), cur(D_SSM), cur(LANES),
            pl.BlockSpec((None, AUX_ROWS, TOKEN_TILE),
                         lambda i: (jnp.minimum(i, n_tiles - 1) // tps, 0, jnp.minimum(i, n_tiles - 1) % tps)),
            _const_spec((1, D_SSM)), _const_spec((1, D_SSM)),
            prev(D_MODEL), prev(D_FOX), prev(D_PLE),
            _const_spec((D_SSM, D_MODEL)), _const_spec((D_FOX, D_MODEL)),
            _const_spec((1, D_MODEL)),
            _const_spec((D_MODEL, D_FF)), _const_spec((D_MODEL, D_FF)), _const_spec((D_FF, D_MODEL)),
            _const_spec((1, D_MODEL)), _const_spec((D_MODEL, D_MODEL)),
            _const_spec((D_PLE, D_MODEL)), _const_spec((1, D_MODEL)),
        ],
        out_specs=prev(D_MODEL),
        out_shape=jax.ShapeDtypeStruct((b, s, D_MODEL), F32),
        scratch_shapes=[pltpu.VMEM((TOKEN_TILE, D_FF), BF16),
                        pltpu.VMEM((N_HEADS // 2, D_STATE, LANES), F32),
                        pltpu.VMEM((2, TOKEN_TILE, D_SSM), BF16)],
        compiler_params=pltpu.CompilerParams(
            dimension_semantics=("arbitrary",), vmem_limit_bytes=VMEM_LIMIT),
        name="ssd_tail",
    )(xbc, z, aux, auxt, dskip_row, snw, h1, yf, p3d, wos, wof, n2, wg, wu, wd, pn, pg, pu, fn)


def _pad_lanes(v):
    return jnp.pad(v, (0, LANES - v.shape[0])).reshape(1, LANES).astype(F32)


def kernel(x, p, ffn1_norm, ffn1_w_gate, ffn1_w_up, ffn1_w_down, mix_norm, w_in, conv_w, conv_b, dt_bias, a_log, d_skip, ssd_norm, fox_f_bias, fox_norm, w_out, ffn2_norm, ffn2_w_gate, ffn2_w_up, ffn2_w_down, ple_norm, ple_w_gate, ple_w_up, final_norm):
    bsz, seq, _ = x.shape
    depth = p.shape[0]
    row = lambda v: v.reshape(1, -1).astype(F32)
    h = x.reshape(bsz * seq, D_MODEL)
    for i in range(depth):
        h1 = _ffn1(h, row(ffn1_norm[i]), ffn1_w_gate[i].astype(BF16), ffn1_w_up[i].astype(BF16),
                   ffn1_w_down[i].astype(BF16))

        w = w_in[i]
        o_xbc, o_dt = D_SSM + D_CONV, D_SSM + D_CONV + N_HEADS
        o_q, o_v = o_dt + D_FOX, o_dt + 3 * D_FOX
        w_dt, w_f = w[:, o_xbc:o_dt], w[:, o_v:]
        gates = jnp.concatenate(
            [w_dt, w_dt, w_f, jnp.zeros((D_MODEL, LANES - 3 * N_HEADS), w.dtype)], axis=1)
        win = jnp.concatenate(
            [w[:, :o_xbc], w[:, o_dt:o_q] * (HEAD_DIM ** -0.5 * LOG2_E), w[:, o_q:o_v], gates],
            axis=1).astype(BF16)
        bias_row = _pad_lanes(jnp.concatenate([dt_bias[i], dt_bias[i], fox_f_bias[i]]))
        a_neg = -jnp.exp(a_log[i].astype(F32))
        a_row = _pad_lanes(jnp.concatenate([jnp.zeros_like(a_neg), a_neg]))
        z, xbc, q, k, v, aux, auxt = _inproj(
            h1.reshape(bsz, seq, D_MODEL), row(mix_norm[i]), win,
            conv_w[i].astype(F32), row(conv_b[i]), bias_row, a_row)

        y_fox = _fox(q, k, v, auxt, row(fox_norm[i]))

        wo = w_out[i].astype(BF16)
        assert depth == 1
        h = _ssd_tail(xbc, z, aux, auxt, row(jnp.repeat(d_skip[i], HEAD_DIM)), row(ssd_norm[i]),
                      h1.reshape(bsz, seq, D_MODEL), y_fox, p[i], wo[:D_SSM], wo[D_SSM:],
                      row(ffn2_norm[i]), ffn2_w_gate[i].astype(BF16), ffn2_w_up[i].astype(BF16),
                      ffn2_w_down[i].astype(BF16), row(ple_norm[i]), ple_w_gate[i].astype(BF16),
                      ple_w_up[i].astype(BF16), row(final_norm))
    return h
```
